```python
import math
import jax, jax.numpy as jnp
from jax import lax
import numpy as np

D_MODEL = 1024
BATCH = 4
SEQ = 8192
DEPTH = 2

D_MIX = D_MODEL
HEAD_DIM = 64
D_SC = D_MIX // 4
D_ATT = D_MIX // 2
D_CC = D_MIX // 4
N_Q_HEADS = D_ATT // HEAD_DIM
N_KV_HEADS = 2
GQA_GROUP = N_Q_HEADS // N_KV_HEADS
SC_WIDTH = 3
CC_WIDTH = 31
WINDOW = 128
BLOCK = 128
ROPE_THETA = 10000.0
D_FF = 2816
LN_EPS = 1e-5
ALPHA = (2.0 * DEPTH) ** 0.25
BETA = (8.0 * DEPTH) ** -0.25
IN_SIZES = (D_SC, D_SC, D_SC, N_Q_HEADS * HEAD_DIM, N_KV_HEADS * HEAD_DIM, N_KV_HEADS * HEAD_DIM, D_CC, D_CC)
D_IN = sum(IN_SIZES)
IN_OFFSETS = tuple(int(v) for v in np.cumsum(IN_SIZES)[:-1])

kernel_name = "hybrid_parallel_conv_swa_conformer_encoder"


def layer_norm(x, g, b):
    xf = x.astype(jnp.float32)
    mu = jnp.mean(xf, axis=-1, keepdims=True)
    var = jnp.mean(jnp.square(xf - mu), axis=-1, keepdims=True)
    y = (xf - mu) * lax.rsqrt(var + LN_EPS)
    return (y * g.astype(jnp.float32) + b.astype(jnp.float32)).astype(x.dtype)


def swiglu(x, w_gu, w_down):
    g, u = jnp.split(x @ w_gu, 2, axis=-1)
    return (jax.nn.silu(g) * u) @ w_down


def depthwise_conv(x, w):
    k = w.shape[0]
    pad = (k - 1) // 2
    return lax.conv_general_dilated(
        x, w[:, None, :].astype(x.dtype), window_strides=(1,), padding=[(pad, pad)],
        dimension_numbers=("NWC", "WIO", "NWC"), feature_group_count=x.shape[-1])


def rope(x, positions):
    half = HEAD_DIM // 2
    inv_freq = ROPE_THETA ** (-jnp.arange(half, dtype=jnp.float32) / half)
    ang = positions.astype(jnp.float32)[:, None] * inv_freq[None, :]
    cos = jnp.cos(ang)[None, :, None, :]
    sin = jnp.sin(ang)[None, :, None, :]
    xf = x.astype(jnp.float32)
    x1, x2 = xf[..., :half], xf[..., half:]
    return jnp.concatenate([x1 * cos - x2 * sin, x2 * cos + x1 * sin], axis=-1).astype(x.dtype)


def banded_window_attention(q, k, v, sink):
    b, s = q.shape[0], q.shape[1]
    nb = s // BLOCK
    qb = q.reshape(b, nb, BLOCK, N_KV_HEADS, GQA_GROUP, HEAD_DIM)

    def band(t):
        tp = jnp.pad(t, ((0, 0), (BLOCK, BLOCK), (0, 0), (0, 0)))
        parts = [tp[:, o * BLOCK:o * BLOCK + s].reshape(b, nb, BLOCK, N_KV_HEADS, HEAD_DIM) for o in range(3)]
        return jnp.concatenate(parts, axis=2)

    kb, vb = band(k), band(v)
    scores = jnp.einsum("bnqkgd,bnskd->bnkgqs", qb, kb).astype(jnp.float32) * (HEAD_DIM ** -0.5)
    qpos = jnp.arange(s).reshape(nb, BLOCK)
    kpos = jnp.arange(nb)[:, None] * BLOCK - BLOCK + jnp.arange(3 * BLOCK)[None, :]
    valid = (jnp.abs(qpos[:, :, None] - kpos[:, None, :]) <= WINDOW) \
        & (kpos >= 0)[:, None, :] & (kpos < s)[:, None, :]
    scores = jnp.where(valid[None, :, None, None], scores, -1e30)
    sink_f = sink.astype(jnp.float32).reshape(N_KV_HEADS, GQA_GROUP)[None, None, :, :, None]
    m = jnp.maximum(jnp.max(scores, axis=-1), sink_f)
    p = jnp.exp(scores - m[..., None])
    denom = jnp.sum(p, axis=-1) + jnp.exp(sink_f - m)
    o = jnp.einsum("bnkgqs,bnskd->bnqkgd", p.astype(v.dtype), vb).astype(jnp.float32)
    o = o / jnp.transpose(denom, (0, 1, 4, 2, 3))[..., None]
    return o.reshape(b, s, D_ATT).astype(q.dtype)


def hybrid_mixer(x, w_in, sc_conv_w, attn_sink, cc_conv_w, cc_conv_b, cc_ln_g, cc_ln_b, w_out):
    b, s, _ = x.shape
    positions = jnp.arange(s)
    z = x @ w_in
    sc_b, sc_c, sc_h, q, k, v, cc_a, cc_gate = jnp.split(z, IN_OFFSETS, axis=-1)
    y_sc = sc_b * depthwise_conv(sc_c * sc_h, sc_conv_w)
    q = rope(q.reshape(b, s, N_Q_HEADS, HEAD_DIM), positions)
    k = rope(k.reshape(b, s, N_KV_HEADS, HEAD_DIM), positions)
    v = v.reshape(b, s, N_KV_HEADS, HEAD_DIM)
    y_att = banded_window_attention(q, k, v, attn_sink)
    u = cc_a * jax.nn.sigmoid(cc_gate)
    u = depthwise_conv(u, cc_conv_w) + cc_conv_b
    y_cc = jax.nn.silu(layer_norm(u, cc_ln_g, cc_ln_b))
    return jnp.concatenate([y_sc, y_att, y_cc], axis=-1) @ w_out


def setup_inputs(seed: int = 0) -> dict:
    key = jax.random.key(seed)
    ks = jax.random.split(key, 24)
    f32 = jnp.float32

    def nrm(k, shape, scale):
        return jax.random.normal(k, shape, f32) * scale

    def gain(k, n):
        return 1.0 + 0.02 * jax.random.normal(k, (DEPTH, n), f32)

    return {
        "x": jax.random.normal(ks[0], (BATCH, SEQ, D_MODEL), f32),
        "ffn1_w_gu": nrm(ks[1], (DEPTH, D_MODEL, 2 * D_FF), D_MODEL ** -0.5),
        "ffn1_w_down": nrm(ks[2], (DEPTH, D_FF, D_MODEL), BETA * D_FF ** -0.5),
        "ln1_g": gain(ks[3], D_MODEL),
        "ln1_b": nrm(ks[4], (DEPTH, D_MODEL), 0.02),
        "w_in": nrm(ks[5], (DEPTH, D_MODEL, D_IN), D_MODEL ** -0.5),
        "sc_conv_w": nrm(ks[6], (DEPTH, SC_WIDTH, D_SC), SC_WIDTH ** -0.5),
        "attn_sink": nrm(ks[7], (DEPTH, N_Q_HEADS), 0.5),
        "cc_conv_w": nrm(ks[8], (DEPTH, CC_WIDTH, D_CC), CC_WIDTH ** -0.5),
        "cc_conv_b": nrm(ks[9], (DEPTH, D_CC), 0.02),
        "cc_ln_g": gain(ks[10], D_CC),
        "cc_ln_b": nrm(ks[11], (DEPTH, D_CC), 0.02),
        "w_out": nrm(ks[12], (DEPTH, D_MIX, D_MODEL), BETA * D_MIX ** -0.5),
        "ln2_g": gain(ks[13], D_MODEL),
        "ln2_b": nrm(ks[14], (DEPTH, D_MODEL), 0.02),
        "ffn2_w_gu": nrm(ks[15], (DEPTH, D_MODEL, 2 * D_FF), D_MODEL ** -0.5),
        "ffn2_w_down": nrm(ks[16], (DEPTH, D_FF, D_MODEL), BETA * D_FF ** -0.5),
        "ln3_g": gain(ks[17], D_MODEL),
        "ln3_b": nrm(ks[18], (DEPTH, D_MODEL), 0.02),
    }


def reference(x, ffn1_w_gu, ffn1_w_down, ln1_g, ln1_b, w_in, sc_conv_w, attn_sink, cc_conv_w,
              cc_conv_b, cc_ln_g, cc_ln_b, w_out, ln2_g, ln2_b, ffn2_w_gu, ffn2_w_down, ln3_g, ln3_b):
    for l in range(DEPTH):
        x = layer_norm(ALPHA * x + 0.5 * swiglu(x, ffn1_w_gu[l], ffn1_w_down[l]), ln1_g[l], ln1_b[l])
        x = layer_norm(ALPHA * x + hybrid_mixer(x, w_in[l], sc_conv_w[l], attn_sink[l], cc_conv_w[l],
                                                 cc_conv_b[l], cc_ln_g[l], cc_ln_b[l], w_out[l]),
                       ln2_g[l], ln2_b[l])
        x = layer_norm(ALPHA * x + 0.5 * swiglu(x, ffn2_w_gu[l], ffn2_w_down[l]), ln3_g[l], ln3_b[l])
    return x
```

```python
import functools

import jax
import jax.numpy as jnp
from jax import lax
from jax.experimental import pallas as pl
from jax.experimental.pallas import tpu as pltpu

D_MODEL = 1024
DEPTH = 2
HEAD_DIM = 64
D_SC = 256
D_ATT = 512
D_CC = 256
N_Q_HEADS = 8
N_KV_HEADS = 2
GQA_GROUP = N_Q_HEADS // N_KV_HEADS
SC_WIDTH = 3
CC_WIDTH = 31
CC_PAD = (CC_WIDTH - 1) // 2
WINDOW = 128
BLOCK = 128
ROPE_THETA = 10000.0
D_FF = 2816
LN_EPS = 1e-5
ALPHA = (2.0 * DEPTH) ** 0.25
D_IN = 3 * D_SC + D_ATT + 2 * N_KV_HEADS * HEAD_DIM + 2 * D_CC
MASK_VALUE = -1e30

OFF_SC_B = 0
OFF_SC_C = D_SC
OFF_SC_H = 2 * D_SC
OFF_Q = 3 * D_SC
OFF_K = OFF_Q + D_ATT
OFF_V = OFF_K + N_KV_HEADS * HEAD_DIM
OFF_CC_A = OFF_V + N_KV_HEADS * HEAD_DIM
OFF_CC_G = OFF_CC_A + D_CC

LANES = 128
HALO_ROWS = 16
VMEM_LIMIT_BYTES = 56 * 1024 * 1024

FFN_ROWS = 512
FFN_CHUNK = 256
MIX_ROWS = 512
CONV_ROWS = 64

F32 = jnp.float32
BF16 = jnp.bfloat16


def _layer_norm(y, g, b):
    mu = jnp.mean(y, axis=-1, keepdims=True)
    yc = y - mu
    var = jnp.mean(yc * yc, axis=-1, keepdims=True)
    return yc * lax.rsqrt(var + LN_EPS) * g + b


def _const_spec(shape):
    return pl.BlockSpec(shape, lambda i: (0,) * len(shape))


def _ffn_body(x_ref, wgu_ref, wd_ref, g_ref, b_ref, o_ref):
    x = x_ref[...]
    xb = x.astype(BF16)
    acc = None
    for c in range(D_FF // FFN_CHUNK):
        gu = jnp.dot(xb, wgu_ref[:, 2 * c * FFN_CHUNK:2 * (c + 1) * FFN_CHUNK],
                     preferred_element_type=F32)
        g = gu[:, :FFN_CHUNK]
        u = gu[:, FFN_CHUNK:]
        h = (g * jax.nn.sigmoid(g) * u).astype(BF16)
        d = jnp.dot(h, wd_ref[c * FFN_CHUNK:(c + 1) * FFN_CHUNK, :], preferred_element_type=F32)
        acc = d if acc is None else acc + d
    y = ALPHA * x + 0.5 * acc
    o_ref[...] = _layer_norm(y, g_ref[...], b_ref[...])


def _ffn(x, wgu, wd, g, b):
    rows = x.shape[0]
    return pl.pallas_call(
        _ffn_body,
        grid=(rows // FFN_ROWS,),
        in_specs=[
            pl.BlockSpec((FFN_ROWS, D_MODEL), lambda i: (i, 0)),
            _const_spec((D_MODEL, 2 * D_FF)),
            _const_spec((D_FF, D_MODEL)),
            _const_spec((1, D_MODEL)),
            _const_spec((1, D_MODEL)),
        ],
        out_specs=pl.BlockSpec((FFN_ROWS, D_MODEL), lambda i: (i, 0)),
        out_shape=jax.ShapeDtypeStruct((rows, D_MODEL), F32),
        compiler_params=pltpu.CompilerParams(
            dimension_semantics=("arbitrary",), vmem_limit_bytes=VMEM_LIMIT_BYTES),
        name="ffn",
    )(x, wgu, wd, g, b)


def _inproj_body(x_ref, w_ref, cos_ref, sin_ref, sb_ref, cv_ref, q_ref, kv_ref):
    xb = x_ref[...].astype(BF16)
    z = jnp.dot(xb, w_ref[...], preferred_element_type=F32)
    sb_ref[...] = z[:, OFF_SC_B:OFF_SC_B + D_SC]
    cv_ref[:, 0:D_SC] = z[:, OFF_SC_C:OFF_SC_C + D_SC] * z[:, OFF_SC_H:OFF_SC_H + D_SC]
    cv_ref[:, D_SC:D_SC + D_CC] = (z[:, OFF_CC_A:OFF_CC_A + D_CC]
                                   * jax.nn.sigmoid(z[:, OFF_CC_G:OFF_CC_G + D_CC]))
    cos = cos_ref[...]
    sin = sin_ref[...]
    lane = lax.broadcasted_iota(jnp.int32, cos.shape, 1)
    first_half = (lane % HEAD_DIM) < (HEAD_DIM // 2)

    def rope(t):
        rot = jnp.where(first_half, pltpu.roll(t, LANES - HEAD_DIM // 2, 1),
                        pltpu.roll(t, HEAD_DIM // 2, 1))
        return t * cos + rot * sin

    for c in range(D_ATT // LANES):
        t = z[:, OFF_Q + c * LANES:OFF_Q + (c + 1) * LANES]
        q_ref[:, c * LANES:(c + 1) * LANES] = (rope(t) * (HEAD_DIM ** -0.5)).astype(BF16)
    kv_ref[:, 0:LANES] = rope(z[:, OFF_K:OFF_K + LANES]).astype(BF16)
    kv_ref[:, LANES:2 * LANES] = z[:, OFF_V:OFF_V + LANES].astype(BF16)


def _inproj(x, w, cos, sin, seq):
    rows = x.shape[0]
    tiles_per_seq = seq // FFN_ROWS
    row_spec = lambda width: pl.BlockSpec((FFN_ROWS, width), lambda i: (i, 0))
    pos_spec = pl.BlockSpec((FFN_ROWS, LANES), lambda i: (i % tiles_per_seq, 0))
    return pl.pallas_call(
        _inproj_body,
        grid=(rows // FFN_ROWS,),
        in_specs=[row_spec(D_MODEL), _const_spec((D_MODEL, D_IN)), pos_spec, pos_spec],
        out_specs=[row_spec(D_SC), row_spec(D_SC + D_CC), row_spec(D_ATT), row_spec(2 * LANES)],
        out_shape=[
            jax.ShapeDtypeStruct((rows, D_SC), F32),
            jax.ShapeDtypeStruct((rows, D_SC + D_CC), F32),
            jax.ShapeDtypeStruct((rows, D_ATT), BF16),
            jax.ShapeDtypeStruct((rows, 2 * LANES), BF16),
        ],
        compiler_params=pltpu.CompilerParams(
            dimension_semantics=("arbitrary",), vmem_limit_bytes=VMEM_LIMIT_BYTES),
        name="inproj",
    )(x, w, cos, sin)


def _mixer_body(sink_ref, x_ref, sb_ref, cvp_ref, cv_ref, cvn_ref, q_ref, kvp_ref, kv_ref, kvn_ref,
                scw_ref, ccw_ref, ccb_ref, ccg_ref, ccbeta_ref, wout_ref, g_ref, b_ref,
                o_ref, cvx_ref, kvx_ref, ycat_ref, *, tiles_per_seq):
    it = pl.program_id(0) % tiles_per_seq
    first = it == 0
    last = it == tiles_per_seq - 1
    n_blocks = MIX_ROWS // BLOCK

    cvx_ref[0:HALO_ROWS, :] = jnp.where(first, 0.0, cvp_ref[...])
    cvx_ref[HALO_ROWS:HALO_ROWS + MIX_ROWS, :] = cv_ref[...]
    cvx_ref[HALO_ROWS + MIX_ROWS:, :] = jnp.where(last, 0.0, cvn_ref[...])
    kvx_ref[0:BLOCK, :] = kvp_ref[...]
    kvx_ref[BLOCK:BLOCK + MIX_ROWS, :] = kv_ref[...]
    kvx_ref[BLOCK + MIX_ROWS:, :] = kvn_ref[...]

    ccb = ccb_ref[...]
    ccg = ccg_ref[...]
    ccbeta = ccbeta_ref[...]
    for r in range(MIX_ROWS // CONV_ROWS):
        base = HALO_ROWS + r * CONV_ROWS
        acc = None
        for k in range(SC_WIDTH):
            tap = cvx_ref[base + k - 1:base + k - 1 + CONV_ROWS, 0:D_SC] * scw_ref[k:k + 1, :]
            acc = tap if acc is None else acc + tap
        y_sc = sb_ref[r * CONV_ROWS:(r + 1) * CONV_ROWS, :] * acc
        ycat_ref[r * CONV_ROWS:(r + 1) * CONV_ROWS, 0:D_SC] = y_sc.astype(BF16)
        acc = None
        for k in range(CC_WIDTH):
            lo = base + k - CC_PAD
            tap = cvx_ref[lo:lo + CONV_ROWS, D_SC:D_SC + D_CC] * ccw_ref[k:k + 1, :]
            acc = tap if acc is None else acc + tap
        u = _layer_norm(acc + ccb, ccg, ccbeta)
        y_cc = u * jax.nn.sigmoid(u)
        ycat_ref[r * CONV_ROWS:(r + 1) * CONV_ROWS, D_SC + D_ATT:] = y_cc.astype(BF16)

    qi = lax.broadcasted_iota(jnp.int32, (BLOCK, 3 * BLOCK), 0)
    kj = lax.broadcasted_iota(jnp.int32, (BLOCK, 3 * BLOCK), 1)
    band = (kj >= qi) & (kj <= qi + 2 * WINDOW)
    key_lo = jnp.where(first, BLOCK, 0)
    key_hi = jnp.where(last, 2 * BLOCK, 3 * BLOCK)
    for j in range(n_blocks):
        valid = band
        if j == 0:
            valid = valid & (kj >= key_lo)
        if j == n_blocks - 1:
            valid = valid & (kj < key_hi)
        for kvh in range(N_KV_HEADS):
            k_blk = kvx_ref[j * BLOCK:(j + 3) * BLOCK, kvh * HEAD_DIM:(kvh + 1) * HEAD_DIM]
            v_blk = kvx_ref[j * BLOCK:(j + 3) * BLOCK,
                            LANES + kvh * HEAD_DIM:LANES + (kvh + 1) * HEAD_DIM]
            for gq in range(GQA_GROUP):
                h = kvh * GQA_GROUP + gq
                q_h = q_ref[j * BLOCK:(j + 1) * BLOCK, h * HEAD_DIM:(h + 1) * HEAD_DIM]
                s = lax.dot_general(q_h, k_blk, (((1,), (1,)), ((), ())),
                                    preferred_element_type=F32)
                s = jnp.where(valid, s, MASK_VALUE)
                sink = sink_ref[h]
                m = jnp.maximum(jnp.max(s, axis=-1, keepdims=True), sink)
                p = jnp.exp(s - m)
                denom = jnp.sum(p, axis=-1, keepdims=True) + jnp.exp(sink - m)
                o = jnp.dot(p.astype(BF16), v_blk, preferred_element_type=F32)
                o = o * (1.0 / denom)
                ycat_ref[j * BLOCK:(j + 1) * BLOCK,
                         D_SC + h * HEAD_DIM:D_SC + (h + 1) * HEAD_DIM] = o.astype(BF16)

    mixed = jnp.dot(ycat_ref[...], wout_ref[...], preferred_element_type=F32)
    o_ref[...] = _layer_norm(ALPHA * x_ref[...] + mixed, g_ref[...], b_ref[...])


def _mixer(x, sb, cv, q, kv, sink, scw, ccw, ccb, ccg, ccbeta, wout, g, b, seq):
    rows = x.shape[0]
    tiles_per_seq = seq // MIX_ROWS
    n_tiles = rows // MIX_ROWS
    halo_per_tile = MIX_ROWS // HALO_ROWS
    blocks_per_tile = MIX_ROWS // BLOCK
    row_spec = lambda width: pl.BlockSpec((MIX_ROWS, width), lambda i: (i, 0))
    cv_prev = pl.BlockSpec((HALO_ROWS, D_SC + D_CC),
                           lambda i: (jnp.maximum(i * halo_per_tile - 1, 0), 0))
    cv_next = pl.BlockSpec((HALO_ROWS, D_SC + D_CC),
                           lambda i: (jnp.minimum((i + 1) * halo_per_tile, n_tiles * halo_per_tile - 1), 0))
    kv_prev = pl.BlockSpec((BLOCK, 2 * LANES),
                           lambda i: (jnp.maximum(i * blocks_per_tile - 1, 0), 0))
    kv_next = pl.BlockSpec((BLOCK, 2 * LANES),
                           lambda i: (jnp.minimum((i + 1) * blocks_per_tile, n_tiles * blocks_per_tile - 1), 0))
    return pl.pallas_call(
        functools.partial(_mixer_body, tiles_per_seq=tiles_per_seq),
        grid=(n_tiles,),
        in_specs=[
            pl.BlockSpec(memory_space=pltpu.SMEM),
            row_spec(D_MODEL), row_spec(D_SC),
            cv_prev, row_spec(D_SC + D_CC), cv_next,
            row_spec(D_ATT),
            kv_prev, row_spec(2 * LANES), kv_next,
            _const_spec((SC_WIDTH, D_SC)), _const_spec((CC_WIDTH, D_CC)),
            _const_spec((1, D_CC)), _const_spec((1, D_CC)), _const_spec((1, D_CC)),
            _const_spec((D_MODEL, D_MODEL)),
            _const_spec((1, D_MODEL)), _const_spec((1, D_MODEL)),
        ],
        out_specs=row_spec(D_MODEL),
        out_shape=jax.ShapeDtypeStruct((rows, D_MODEL), F32),
        scratch_shapes=[
            pltpu.VMEM((MIX_ROWS + 2 * HALO_ROWS, D_SC + D_CC), F32),
            pltpu.VMEM((MIX_ROWS + 2 * BLOCK, 2 * LANES), BF16),
            pltpu.VMEM((MIX_ROWS, D_MODEL), BF16),
        ],
        compiler_params=pltpu.CompilerParams(
            dimension_semantics=("arbitrary",), vmem_limit_bytes=VMEM_LIMIT_BYTES),
        name="mixer",
    )(sink, x, sb, cv, cv, cv, q, kv, kv, kv, scw, ccw, ccb, ccg, ccbeta, wout, g, b)


def _interleave_gate_up(w_gu):
    d = w_gu.shape[0]
    n = D_FF // FFN_CHUNK
    gate = w_gu[:, :D_FF].reshape(d, n, 1, FFN_CHUNK)
    up = w_gu[:, D_FF:].reshape(d, n, 1, FFN_CHUNK)
    return jnp.concatenate([gate, up], axis=2).reshape(d, 2 * D_FF)


def _rope_tables(seq):
    half = HEAD_DIM // 2
    inv_freq = ROPE_THETA ** (-jnp.arange(half, dtype=F32) / half)
    ang = jnp.arange(seq).astype(F32)[:, None] * inv_freq[None, :]
    cos = jnp.cos(ang)
    sin = jnp.sin(ang)
    reps = LANES // HEAD_DIM
    cos_t = jnp.tile(jnp.concatenate([cos, cos], axis=-1), (1, reps))
    sin_t = jnp.tile(jnp.concatenate([-sin, sin], axis=-1), (1, reps))
    return cos_t, sin_t


def kernel(x, ffn1_w_gu, ffn1_w_down, ln1_g, ln1_b, w_in, sc_conv_w, attn_sink, cc_conv_w, cc_conv_b,
           cc_ln_g, cc_ln_b, w_out, ln2_g, ln2_b, ffn2_w_gu, ffn2_w_down, ln3_g, ln3_b):
    batch, seq, d = x.shape
    assert d == D_MODEL and seq % MIX_ROWS == 0 and seq % FFN_ROWS == 0
    cos_t, sin_t = _rope_tables(seq)
    row = lambda v: v.reshape(1, -1)
    h = x.reshape(batch * seq, d)
    for l in range(DEPTH):
        h = _ffn(h, _interleave_gate_up(ffn1_w_gu[l]).astype(BF16), ffn1_w_down[l].astype(BF16),
                 row(ln1_g[l]), row(ln1_b[l]))
        sb, cv, q, kv = _inproj(h, w_in[l].astype(BF16), cos_t, sin_t, seq)
        h = _mixer(h, sb, cv, q, kv, attn_sink[l], sc_conv_w[l], cc_conv_w[l], row(cc_conv_b[l]),
                   row(cc_ln_g[l]), row(cc_ln_b[l]), w_out[l].astype(BF16),
                   row(ln2_g[l]), row(ln2_b[l]), seq)
        h = _ffn(h, _interleave_gate_up(ffn2_w_gu[l]).astype(BF16), ffn2_w_down[l].astype(BF16),
                 row(ln3_g[l]), row(ln3_b[l]))
    return h.reshape(batch, seq, d)
```

```python
import functools

import jax
import jax.numpy as jnp
from jax import lax
from jax.experimental import pallas as pl
from jax.experimental.pallas import tpu as pltpu

D_MODEL = 1024
DEPTH = 2
HEAD_DIM = 64
D_SC = 256
D_ATT = 512
D_CC = 256
N_Q_HEADS = 8
N_KV_HEADS = 2
GQA_GROUP = N_Q_HEADS // N_KV_HEADS
SC_WIDTH = 3
CC_WIDTH = 31
CC_PAD = (CC_WIDTH - 1) // 2
WINDOW = 128
BLOCK = 128
ROPE_THETA = 10000.0
D_FF = 2816
LN_EPS = 1e-5
ALPHA = (2.0 * DEPTH) ** 0.25
D_IN = 3 * D_SC + D_ATT + 2 * N_KV_HEADS * HEAD_DIM + 2 * D_CC
MASK_VALUE = -1e30

OFF_SC_B = 0
OFF_SC_C = D_SC
OFF_SC_H = 2 * D_SC
OFF_Q = 3 * D_SC
OFF_K = OFF_Q + D_ATT
OFF_V = OFF_K + N_KV_HEADS * HEAD_DIM
OFF_CC_A = OFF_V + N_KV_HEADS * HEAD_DIM
OFF_CC_G = OFF_CC_A + D_CC

LANES = 128
SUBLANES = 8
HALO_ROWS = 16
VMEM_LIMIT_BYTES = 56 * 1024 * 1024

FFN_ROWS = 512
FFN_CHUNK = 256
MIX_ROWS = 512
CONV_ROWS = 64
KV_WIDTH = 2 * N_KV_HEADS * LANES
PAIR_ROWS = 2 * BLOCK

F32 = jnp.float32
BF16 = jnp.bfloat16


def _layer_norm(y, g, b):
    mu = jnp.mean(y, axis=-1, keepdims=True)
    yc = y - mu
    var = jnp.mean(yc * yc, axis=-1, keepdims=True)
    return yc * lax.rsqrt(var + LN_EPS) * g + b


def _const_spec(shape):
    return pl.BlockSpec(shape, lambda i: (0,) * len(shape))


def _resident_spec(shape):
    return pl.BlockSpec(shape, lambda i: (0,) * len(shape), pipeline_mode=pl.Buffered(1))


def _ffn_body(x_ref, wgu_ref, wd_ref, g_ref, b_ref, o_ref):
    x = x_ref[...]
    xb = x.astype(BF16)
    acc = None
    for c in range(D_FF // FFN_CHUNK):
        g = jnp.dot(xb, wgu_ref[:, c * FFN_CHUNK:(c + 1) * FFN_CHUNK], preferred_element_type=F32)
        u = jnp.dot(xb, wgu_ref[:, D_FF + c * FFN_CHUNK:D_FF + (c + 1) * FFN_CHUNK],
                    preferred_element_type=F32)
        h = (g * jax.nn.sigmoid(g) * u).astype(BF16)
        d = jnp.dot(h, wd_ref[c * FFN_CHUNK:(c + 1) * FFN_CHUNK, :], preferred_element_type=F32)
        acc = d if acc is None else acc + d
    y = ALPHA * x + 0.5 * acc
    o_ref[...] = _layer_norm(y, g_ref[...], b_ref[...])


def _ffn(x, wgu, wd, g, b):
    rows = x.shape[0]
    return pl.pallas_call(
        _ffn_body,
        grid=(rows // FFN_ROWS,),
        in_specs=[
            pl.BlockSpec((FFN_ROWS, D_MODEL), lambda i: (i, 0)),
            _resident_spec((D_MODEL, 2 * D_FF)),
            _resident_spec((D_FF, D_MODEL)),
            _const_spec((1, D_MODEL)),
            _const_spec((1, D_MODEL)),
        ],
        out_specs=pl.BlockSpec((FFN_ROWS, D_MODEL), lambda i: (i, 0)),
        out_shape=jax.ShapeDtypeStruct((rows, D_MODEL), F32),
        compiler_params=pltpu.CompilerParams(
            dimension_semantics=("arbitrary",), vmem_limit_bytes=VMEM_LIMIT_BYTES),
        name="ffn",
    )(x, wgu, wd, g, b)


def _inproj_body(x_ref, w_ref, cos_ref, sin_ref, sb_ref, cv_ref, q_ref, kv_ref):
    xb = x_ref[...].astype(BF16)
    z = jnp.dot(xb, w_ref[...], preferred_element_type=F32)
    sb_ref[...] = z[:, OFF_SC_B:OFF_SC_B + D_SC]
    cv_ref[:, 0:D_SC] = z[:, OFF_SC_C:OFF_SC_C + D_SC] * z[:, OFF_SC_H:OFF_SC_H + D_SC]
    cv_ref[:, D_SC:D_SC + D_CC] = (z[:, OFF_CC_A:OFF_CC_A + D_CC]
                                   * jax.nn.sigmoid(z[:, OFF_CC_G:OFF_CC_G + D_CC]))
    cos = cos_ref[...]
    sin = sin_ref[...]
    lane = lax.broadcasted_iota(jnp.int32, cos.shape, 1)
    first_half = (lane % HEAD_DIM) < (HEAD_DIM // 2)
    low_head = lane < HEAD_DIM

    def rope(t):
        rot = jnp.where(first_half, pltpu.roll(t, LANES - HEAD_DIM // 2, 1),
                        pltpu.roll(t, HEAD_DIM // 2, 1))
        return t * cos + rot * sin

    def duplicate_heads(t, o_ref, off):
        swapped = pltpu.roll(t, HEAD_DIM, 1)
        o_ref[:, off:off + LANES] = jnp.where(low_head, t, swapped).astype(BF16)
        o_ref[:, off + LANES:off + 2 * LANES] = jnp.where(low_head, swapped, t).astype(BF16)

    for c in range(D_ATT // LANES):
        t = z[:, OFF_Q + c * LANES:OFF_Q + (c + 1) * LANES]
        q_ref[:, c * LANES:(c + 1) * LANES] = (rope(t) * (HEAD_DIM ** -0.5)).astype(BF16)
    duplicate_heads(rope(z[:, OFF_K:OFF_K + LANES]), kv_ref, 0)
    duplicate_heads(z[:, OFF_V:OFF_V + LANES], kv_ref, N_KV_HEADS * LANES)


def _inproj(x, w, cos, sin, seq):
    rows = x.shape[0]
    tiles_per_seq = seq // FFN_ROWS
    row_spec = lambda width: pl.BlockSpec((FFN_ROWS, width), lambda i: (i, 0))
    pos_spec = pl.BlockSpec((FFN_ROWS, LANES), lambda i: (i % tiles_per_seq, 0))
    return pl.pallas_call(
        _inproj_body,
        grid=(rows // FFN_ROWS,),
        in_specs=[row_spec(D_MODEL), _resident_spec((D_MODEL, D_IN)), pos_spec, pos_spec],
        out_specs=[row_spec(D_SC), row_spec(D_SC + D_CC), row_spec(D_ATT), row_spec(KV_WIDTH)],
        out_shape=[
            jax.ShapeDtypeStruct((rows, D_SC), F32),
            jax.ShapeDtypeStruct((rows, D_SC + D_CC), F32),
            jax.ShapeDtypeStruct((rows, D_ATT), BF16),
            jax.ShapeDtypeStruct((rows, KV_WIDTH), BF16),
        ],
        compiler_params=pltpu.CompilerParams(
            dimension_semantics=("arbitrary",), vmem_limit_bytes=VMEM_LIMIT_BYTES),
        name="inproj",
    )(x, w, cos, sin)


def _mixer_body(sink_ref, x_ref, sb_ref, cvp_ref, cv_ref, cvn_ref, q_ref, kvp_ref, kv_ref, kvn_ref,
                scw_ref, ccw_ref, ccb_ref, ccg_ref, ccbeta_ref, wout_ref, g_ref, b_ref,
                o_ref, cvx_ref, shf_ref, kvx_ref, ycat_ref, *, tiles_per_seq):
    it = pl.program_id(0) % tiles_per_seq
    first = it == 0
    last = it == tiles_per_seq - 1
    n_blocks = MIX_ROWS // BLOCK
    ext_rows = MIX_ROWS + 2 * HALO_ROWS

    cvx_ref[0:HALO_ROWS, :] = jnp.where(first, 0.0, cvp_ref[...])
    cvx_ref[HALO_ROWS:HALO_ROWS + MIX_ROWS, :] = cv_ref[...]
    cvx_ref[HALO_ROWS + MIX_ROWS:, :] = jnp.where(last, 0.0, cvn_ref[...])
    ext_u = cvx_ref[:, D_SC:]
    for r in range(1, SUBLANES):
        shf_ref[r - 1, :, D_SC:] = pltpu.roll(ext_u, ext_rows - r, 0)
    ext_p = cvx_ref[:, :D_SC]
    for r in sorted({(HALO_ROWS + k - 1) % SUBLANES for k in range(SC_WIDTH)} - {0}):
        shf_ref[r - 1, :, :D_SC] = pltpu.roll(ext_p, ext_rows - r, 0)

    def shifted(row, lo_lane, hi_lane):
        r = row % SUBLANES
        if r == 0:
            return cvx_ref[row:row + CONV_ROWS, lo_lane:hi_lane]
        return shf_ref[r - 1, row - r:row - r + CONV_ROWS, lo_lane:hi_lane]

    lane = lax.broadcasted_iota(jnp.int32, (BLOCK, LANES), 1)
    low = lane < HEAD_DIM
    for src_ref, row0, n in ((kvp_ref, 0, BLOCK), (kv_ref, BLOCK, MIX_ROWS),
                             (kvn_ref, BLOCK + MIX_ROWS, BLOCK)):
        for rb in range(n // BLOCK):
            for c in range(KV_WIDTH // LANES):
                blk = src_ref[rb * BLOCK:(rb + 1) * BLOCK, c * LANES:(c + 1) * LANES]
                rows = slice(row0 + rb * BLOCK, row0 + (rb + 1) * BLOCK)
                kvx_ref[rows, 2 * c * LANES:(2 * c + 1) * LANES] = jnp.where(low, blk, 0)
                kvx_ref[rows, (2 * c + 1) * LANES:(2 * c + 2) * LANES] = jnp.where(low, 0, blk)

    ccb = ccb_ref[...]
    ccg = ccg_ref[...]
    ccbeta = ccbeta_ref[...]
    for rc in range(MIX_ROWS // CONV_ROWS):
        base = HALO_ROWS + rc * CONV_ROWS
        acc = None
        for k in range(SC_WIDTH):
            tap = shifted(base + k - 1, 0, D_SC) * scw_ref[k:k + 1, :]
            acc = tap if acc is None else acc + tap
        y_sc = sb_ref[rc * CONV_ROWS:(rc + 1) * CONV_ROWS, :] * acc
        ycat_ref[rc * CONV_ROWS:(rc + 1) * CONV_ROWS, 0:D_SC] = y_sc.astype(BF16)
        acc = None
        for k in range(CC_WIDTH):
            tap = shifted(base + k - CC_PAD, D_SC, D_SC + D_CC) * ccw_ref[k:k + 1, :]
            acc = tap if acc is None else acc + tap
        u = _layer_norm(acc + ccb, ccg, ccbeta)
        y_cc = u * jax.nn.sigmoid(u)
        ycat_ref[rc * CONV_ROWS:(rc + 1) * CONV_ROWS, D_SC + D_ATT:] = y_cc.astype(BF16)

    qi = lax.broadcasted_iota(jnp.int32, (BLOCK, 3 * BLOCK), 0)
    kj = lax.broadcasted_iota(jnp.int32, (BLOCK, 3 * BLOCK), 1)
    band = (kj >= qi) & (kj <= qi + 2 * WINDOW)
    key_lo = jnp.where(first, BLOCK, 0)
    key_hi = jnp.where(last, 2 * BLOCK, 3 * BLOCK)
    low_pair = lax.broadcasted_iota(jnp.int32, (BLOCK, LANES), 1) < HEAD_DIM
    contract_lanes = (((1,), (1,)), ((), ()))
    for j in range(n_blocks):
        valid = band
        if j == 0:
            valid = valid & (kj >= key_lo)
        if j == n_blocks - 1:
            valid = valid & (kj < key_hi)
        q_rows = slice(j * BLOCK, (j + 1) * BLOCK)
        k_rows = slice(j * BLOCK, (j + 3) * BLOCK)
        for kvh in range(N_KV_HEADS):
            qp = jnp.concatenate(
                [q_ref[q_rows, (2 * kvh) * LANES:(2 * kvh + 1) * LANES],
                 q_ref[q_rows, (2 * kvh + 1) * LANES:(2 * kvh + 2) * LANES]], axis=0)
            probs = []
            rinv = []
            for parity in range(2):
                k_blk = kvx_ref[k_rows, (2 * kvh + parity) * LANES:(2 * kvh + parity + 1) * LANES]
                s_all = lax.dot_general(qp, k_blk, contract_lanes, preferred_element_type=F32)
                p_slabs = []
                for pair in range(2):
                    h = GQA_GROUP * kvh + 2 * pair + parity
                    s = jnp.where(valid, s_all[pair * BLOCK:(pair + 1) * BLOCK, :], MASK_VALUE)
                    sink = sink_ref[h]
                    m = jnp.maximum(jnp.max(s, axis=-1, keepdims=True), sink)
                    p = jnp.exp(s - m)
                    denom = jnp.sum(p, axis=-1, keepdims=True) + jnp.exp(sink - m)
                    rinv.append(1.0 / denom)
                    p_slabs.append(p.astype(BF16))
                probs.append(jnp.concatenate(p_slabs, axis=0))
            v_off = 2 * N_KV_HEADS * LANES + 2 * kvh * LANES
            o = (jnp.dot(probs[0], kvx_ref[k_rows, v_off:v_off + LANES], preferred_element_type=F32)
                 + jnp.dot(probs[1], kvx_ref[k_rows, v_off + LANES:v_off + 2 * LANES],
                           preferred_element_type=F32))
            for pair in range(2):
                scale = jnp.where(low_pair, rinv[pair], rinv[2 + pair])
                col = D_SC + (2 * kvh + pair) * LANES
                ycat_ref[q_rows, col:col + LANES] = (
                    o[pair * BLOCK:(pair + 1) * BLOCK, :] * scale).astype(BF16)

    mixed = jnp.dot(ycat_ref[...], wout_ref[...], preferred_element_type=F32)
    o_ref[...] = _layer_norm(ALPHA * x_ref[...] + mixed, g_ref[...], b_ref[...])


def _mixer(x, sb, cv, q, kv, sink, scw, ccw, ccb, ccg, ccbeta, wout, g, b, seq):
    rows = x.shape[0]
    tiles_per_seq = seq // MIX_ROWS
    n_tiles = rows // MIX_ROWS
    halo_per_tile = MIX_ROWS // HALO_ROWS
    blocks_per_tile = MIX_ROWS // BLOCK
    row_spec = lambda width: pl.BlockSpec((MIX_ROWS, width), lambda i: (i, 0))
    cv_prev = pl.BlockSpec((HALO_ROWS, D_SC + D_CC),
                           lambda i: (jnp.maximum(i * halo_per_tile - 1, 0), 0))
    cv_next = pl.BlockSpec((HALO_ROWS, D_SC + D_CC),
                           lambda i: (jnp.minimum((i + 1) * halo_per_tile, n_tiles * halo_per_tile - 1), 0))
    kv_prev = pl.BlockSpec((BLOCK, KV_WIDTH),
                           lambda i: (jnp.maximum(i * blocks_per_tile - 1, 0), 0))
    kv_next = pl.BlockSpec((BLOCK, KV_WIDTH),
                           lambda i: (jnp.minimum((i + 1) * blocks_per_tile, n_tiles * blocks_per_tile - 1), 0))
    ext_rows = MIX_ROWS + 2 * HALO_ROWS
    return pl.pallas_call(
        functools.partial(_mixer_body, tiles_per_seq=tiles_per_seq),
        grid=(n_tiles,),
        in_specs=[
            pl.BlockSpec(memory_space=pltpu.SMEM),
            row_spec(D_MODEL), row_spec(D_SC),
            cv_prev, row_spec(D_SC + D_CC), cv_next,
            row_spec(D_ATT),
            kv_prev, row_spec(KV_WIDTH), kv_next,
            _const_spec((SC_WIDTH, D_SC)), _const_spec((CC_WIDTH, D_CC)),
            _const_spec((1, D_CC)), _const_spec((1, D_CC)), _const_spec((1, D_CC)),
            _resident_spec((D_MODEL, D_MODEL)),
            _const_spec((1, D_MODEL)), _const_spec((1, D_MODEL)),
        ],
        out_specs=row_spec(D_MODEL),
        out_shape=jax.ShapeDtypeStruct((rows, D_MODEL), F32),
        scratch_shapes=[
            pltpu.VMEM((ext_rows, D_SC + D_CC), F32),
            pltpu.VMEM((SUBLANES - 1, ext_rows, D_SC + D_CC), F32),
            pltpu.VMEM((MIX_ROWS + 2 * BLOCK, 2 * KV_WIDTH), BF16),
            pltpu.VMEM((MIX_ROWS, D_MODEL), BF16),
        ],
        compiler_params=pltpu.CompilerParams(
            dimension_semantics=("arbitrary",), vmem_limit_bytes=VMEM_LIMIT_BYTES),
        name="mixer",
    )(sink, x, sb, cv, cv, cv, q, kv, kv, kv, scw, ccw, ccb, ccg, ccbeta, wout, g, b)


def _rope_tables(seq):
    half = HEAD_DIM // 2
    inv_freq = ROPE_THETA ** (-jnp.arange(half, dtype=F32) / half)
    ang = jnp.arange(seq).astype(F32)[:, None] * inv_freq[None, :]
    cos = jnp.cos(ang)
    sin = jnp.sin(ang)
    reps = LANES // HEAD_DIM
    cos_t = jnp.tile(jnp.concatenate([cos, cos], axis=-1), (1, reps))
    sin_t = jnp.tile(jnp.concatenate([-sin, sin], axis=-1), (1, reps))
    return cos_t, sin_t


def kernel(x, ffn1_w_gu, ffn1_w_down, ln1_g, ln1_b, w_in, sc_conv_w, attn_sink, cc_conv_w, cc_conv_b,
           cc_ln_g, cc_ln_b, w_out, ln2_g, ln2_b, ffn2_w_gu, ffn2_w_down, ln3_g, ln3_b):
    batch, seq, d = x.shape
    assert d == D_MODEL and seq % MIX_ROWS == 0 and seq % FFN_ROWS == 0
    cos_t, sin_t = _rope_tables(seq)
    row = lambda v: v.reshape(1, -1)
    h = x.reshape(batch * seq, d)
    for l in range(DEPTH):
        h = _ffn(h, ffn1_w_gu[l].astype(BF16), ffn1_w_down[l].astype(BF16),
                 row(ln1_g[l]), row(ln1_b[l]))
        sb, cv, q, kv = _inproj(h, w_in[l].astype(BF16), cos_t, sin_t, seq)
        h = _mixer(h, sb, cv, q, kv, attn_sink[l], sc_conv_w[l], cc_conv_w[l], row(cc_conv_b[l]),
                   row(cc_ln_g[l]), row(cc_ln_b[l]), w_out[l].astype(BF16),
                   row(ln2_g[l]), row(ln2_b[l]), seq)
        h = _ffn(h, ffn2_w_gu[l].astype(BF16), ffn2_w_down[l].astype(BF16),
                 row(ln3_g[l]), row(ln3_b[l]))
    return h.reshape(batch, seq, d)
```

```python
import functools

import jax
import jax.numpy as jnp
from jax import lax
from jax.experimental import pallas as pl
from jax.experimental.pallas import tpu as pltpu

D_MODEL = 1024
DEPTH = 2
HEAD_DIM = 64
D_SC = 256
D_ATT = 512
D_CC = 256
N_Q_HEADS = 8
N_KV_HEADS = 2
GQA_GROUP = N_Q_HEADS // N_KV_HEADS
SC_WIDTH = 3
CC_WIDTH = 31
CC_PAD = (CC_WIDTH - 1) // 2
WINDOW = 128
BLOCK = 128
ROPE_THETA = 10000.0
D_FF = 2816
LN_EPS = 1e-5
ALPHA = (2.0 * DEPTH) ** 0.25
D_IN = 3 * D_SC + D_ATT + 2 * N_KV_HEADS * HEAD_DIM + 2 * D_CC
MASK_VALUE = -1e30

OFF_SC_B = 0
OFF_SC_C = D_SC
OFF_SC_H = 2 * D_SC
OFF_Q = 3 * D_SC
OFF_K = OFF_Q + D_ATT
OFF_V = OFF_K + N_KV_HEADS * HEAD_DIM
OFF_CC_A = OFF_V + N_KV_HEADS * HEAD_DIM
OFF_CC_G = OFF_CC_A + D_CC

LANES = 128
SUBLANES = 8
HALO_ROWS = 16
VMEM_LIMIT_BYTES = 56 * 1024 * 1024

FFN_ROWS = 512
FFN_CHUNK = 256
MIX_ROWS = 512
CONV_ROWS = 64
KV_WIDTH = 2 * N_KV_HEADS * LANES
ATT_LAG = 2
LOG2_E = 1.4426950408889634

F32 = jnp.float32
BF16 = jnp.bfloat16


def _layer_norm(y, g, b):
    mu = jnp.mean(y, axis=-1, keepdims=True)
    yc = y - mu
    var = jnp.mean(yc * yc, axis=-1, keepdims=True)
    return yc * lax.rsqrt(var + LN_EPS) * g + b


def _const_spec(shape):
    return pl.BlockSpec(shape, lambda i: (0,) * len(shape))


def _layer_weight_spec(layer, shape):
    return pl.BlockSpec((None,) + shape, lambda i: (layer,) + (0,) * len(shape),
                        pipeline_mode=pl.Buffered(1))


def _zeros_from(anchor):
    bits = lax.bitcast_convert_type(anchor, jnp.uint32)
    zero = lax.shift_right_logical(lax.shift_right_logical(bits, jnp.uint32(16)), jnp.uint32(16))
    return lax.bitcast_convert_type(zero, F32)


def _ordered_after(value, anchor):
    return value + _zeros_from(anchor)


def _fold_to_vreg(v):
    acc = v[:, 0:LANES]
    for c in range(1, v.shape[1] // LANES):
        acc = acc + v[:, c * LANES:(c + 1) * LANES]
    out = acc[0:SUBLANES]
    for r in range(1, v.shape[0] // SUBLANES):
        out = out + acc[r * SUBLANES:(r + 1) * SUBLANES]
    return out


def _swiglu_residual(x_ref, wgu_ref, wd_ref, anchor=None):
    x = x_ref[...]
    xb = x.astype(BF16)
    acc = None
    n_chunks = D_FF // FFN_CHUNK
    for c in range(n_chunks):
        g = jnp.dot(xb, wgu_ref[:, c * FFN_CHUNK:(c + 1) * FFN_CHUNK], preferred_element_type=F32)
        u = jnp.dot(xb, wgu_ref[:, D_FF + c * FFN_CHUNK:D_FF + (c + 1) * FFN_CHUNK],
                    preferred_element_type=F32)
        h = g * jax.nn.sigmoid(g) * u
        if anchor is not None and c == n_chunks // 2:
            h = _ordered_after(h, anchor)
        d = jnp.dot(h.astype(BF16), wd_ref[c * FFN_CHUNK:(c + 1) * FFN_CHUNK, :],
                    preferred_element_type=F32)
        acc = d if acc is None else acc + d
    return ALPHA * x + 0.5 * acc


def _ffn_body(x_ref, wgu_ref, wd_ref, g_ref, b_ref, o_ref, y_ref):
    i = pl.program_id(0)
    n_tiles = pl.num_programs(0) - 1

    @pl.when(i == 0)
    def _():
        y_ref[...] = _swiglu_residual(x_ref, wgu_ref, wd_ref)

    @pl.when((i > 0) & (i < n_tiles))
    def _():
        out = _layer_norm(y_ref[...], g_ref[...], b_ref[...])
        o_ref[...] = out
        anchor = out[:, 0:FFN_CHUNK]
        for c in range(1, D_MODEL // FFN_CHUNK):
            anchor = anchor + out[:, c * FFN_CHUNK:(c + 1) * FFN_CHUNK]
        y_ref[...] = _swiglu_residual(x_ref, wgu_ref, wd_ref, anchor)

    @pl.when(i == n_tiles)
    def _():
        o_ref[...] = _layer_norm(y_ref[...], g_ref[...], b_ref[...])


def _ffn(x, wgu, wd, g, b, layer):
    rows = x.shape[0]
    n_tiles = rows // FFN_ROWS
    return pl.pallas_call(
        _ffn_body,
        grid=(n_tiles + 1,),
        in_specs=[
            pl.BlockSpec((FFN_ROWS, D_MODEL), lambda i: (jnp.minimum(i, n_tiles - 1), 0)),
            _layer_weight_spec(layer, (D_MODEL, 2 * D_FF)),
            _layer_weight_spec(layer, (D_FF, D_MODEL)),
            _const_spec((1, D_MODEL)),
            _const_spec((1, D_MODEL)),
        ],
        out_specs=pl.BlockSpec((FFN_ROWS, D_MODEL), lambda i: (jnp.maximum(i - 1, 0), 0)),
        out_shape=jax.ShapeDtypeStruct((rows, D_MODEL), F32),
        scratch_shapes=[pltpu.VMEM((FFN_ROWS, D_MODEL), F32)],
        compiler_params=pltpu.CompilerParams(
            dimension_semantics=("arbitrary",), vmem_limit_bytes=VMEM_LIMIT_BYTES),
        name="ffn",
    )(x, wgu, wd, g, b)


def _inproj_body(x_ref, w_ref, cos_ref, sin_ref, sb_ref, cv_ref, q_ref, kv_ref):
    xb = x_ref[...].astype(BF16)
    z = jnp.dot(xb, w_ref[...], preferred_element_type=F32)
    sb_ref[...] = z[:, OFF_SC_B:OFF_SC_B + D_SC]
    cv_ref[:, 0:D_SC] = z[:, OFF_SC_C:OFF_SC_C + D_SC] * z[:, OFF_SC_H:OFF_SC_H + D_SC]
    cv_ref[:, D_SC:D_SC + D_CC] = (z[:, OFF_CC_A:OFF_CC_A + D_CC]
                                   * jax.nn.sigmoid(z[:, OFF_CC_G:OFF_CC_G + D_CC]))
    cos = cos_ref[...]
    sin = sin_ref[...]
    lane = lax.broadcasted_iota(jnp.int32, cos.shape, 1)
    first_half = (lane % HEAD_DIM) < (HEAD_DIM // 2)
    low_head = lane < HEAD_DIM

    def rope(t):
        rot = jnp.where(first_half, pltpu.roll(t, LANES - HEAD_DIM // 2, 1),
                        pltpu.roll(t, HEAD_DIM // 2, 1))
        return t * cos + rot * sin

    def duplicate_heads(t, o_ref, off):
        swapped = pltpu.roll(t, HEAD_DIM, 1)
        o_ref[:, off:off + LANES] = jnp.where(low_head, t, swapped).astype(BF16)
        o_ref[:, off + LANES:off + 2 * LANES] = jnp.where(low_head, swapped, t).astype(BF16)

    for c in range(D_ATT // LANES):
        t = z[:, OFF_Q + c * LANES:OFF_Q + (c + 1) * LANES]
        q_ref[:, c * LANES:(c + 1) * LANES] = (rope(t) * (HEAD_DIM ** -0.5 * LOG2_E)).astype(BF16)
    duplicate_heads(rope(z[:, OFF_K:OFF_K + LANES]), kv_ref, 0)
    duplicate_heads(z[:, OFF_V:OFF_V + LANES], kv_ref, N_KV_HEADS * LANES)


def _inproj(x, w, cos, sin, seq, layer):
    rows = x.shape[0]
    tiles_per_seq = seq // FFN_ROWS
    row_spec = lambda width: pl.BlockSpec((FFN_ROWS, width), lambda i: (i, 0))
    pos_spec = pl.BlockSpec((FFN_ROWS, LANES), lambda i: (i % tiles_per_seq, 0))
    return pl.pallas_call(
        _inproj_body,
        grid=(rows // FFN_ROWS,),
        in_specs=[row_spec(D_MODEL), _layer_weight_spec(layer, (D_MODEL, D_IN)), pos_spec, pos_spec],
        out_specs=[row_spec(D_SC), row_spec(D_SC + D_CC), row_spec(D_ATT), row_spec(KV_WIDTH)],
        out_shape=[
            jax.ShapeDtypeStruct((rows, D_SC), F32),
            jax.ShapeDtypeStruct((rows, D_SC + D_CC), F32),
            jax.ShapeDtypeStruct((rows, D_ATT), BF16),
            jax.ShapeDtypeStruct((rows, KV_WIDTH), BF16),
        ],
        compiler_params=pltpu.CompilerParams(
            dimension_semantics=("arbitrary",), vmem_limit_bytes=VMEM_LIMIT_BYTES),
        name="inproj",
    )(x, w, cos, sin)


def _mixer_body(sink_ref, x_ref, sb_ref, cvp_ref, cv_ref, cvn_ref, q_ref, kvp_ref, kv_ref, kvn_ref,
                scw_ref, ccw_ref, ccb_ref, ccg_ref, ccbeta_ref, wout_ref, g_ref, b_ref,
                o_ref, cvx_ref, shf_ref, kvx_ref, ycat_ref, *, tiles_per_seq):
    it = pl.program_id(0) % tiles_per_seq
    first = it == 0
    last = it == tiles_per_seq - 1
    n_blocks = MIX_ROWS // BLOCK
    ext_rows = MIX_ROWS + 2 * HALO_ROWS

    cvx_ref[0:HALO_ROWS, :] = jnp.where(first, 0.0, cvp_ref[...])
    cvx_ref[HALO_ROWS:HALO_ROWS + MIX_ROWS, :] = cv_ref[...]
    cvx_ref[HALO_ROWS + MIX_ROWS:, :] = jnp.where(last, 0.0, cvn_ref[...])
    ext_u = cvx_ref[:, D_SC:]
    for r in range(1, SUBLANES):
        shf_ref[r - 1, :, D_SC:] = pltpu.roll(ext_u, ext_rows - r, 0)
    ext_p = cvx_ref[:, :D_SC]
    for r in sorted({(HALO_ROWS + k - 1) % SUBLANES for k in range(SC_WIDTH)} - {0}):
        shf_ref[r - 1, :, :D_SC] = pltpu.roll(ext_p, ext_rows - r, 0)

    def shifted(row, lo_lane, hi_lane):
        r = row % SUBLANES
        if r == 0:
            return cvx_ref[row:row + CONV_ROWS, lo_lane:hi_lane]
        return shf_ref[r - 1, row - r:row - r + CONV_ROWS, lo_lane:hi_lane]

    lane = lax.broadcasted_iota(jnp.int32, (BLOCK, LANES), 1)
    low = lane < HEAD_DIM
    for src_ref, row0, n in ((kvp_ref, 0, BLOCK), (kv_ref, BLOCK, MIX_ROWS),
                             (kvn_ref, BLOCK + MIX_ROWS, BLOCK)):
        for rb in range(n // BLOCK):
            for c in range(KV_WIDTH // LANES):
                blk = src_ref[rb * BLOCK:(rb + 1) * BLOCK, c * LANES:(c + 1) * LANES]
                rows = slice(row0 + rb * BLOCK, row0 + (rb + 1) * BLOCK)
                kvx_ref[rows, 2 * c * LANES:(2 * c + 1) * LANES] = jnp.where(low, blk, 0)
                kvx_ref[rows, (2 * c + 1) * LANES:(2 * c + 2) * LANES] = jnp.where(low, 0, blk)

    ccb = ccb_ref[...]
    ccg = ccg_ref[...]
    ccbeta = ccbeta_ref[...]

    def conv_chunk(rc):
        base = HALO_ROWS + rc * CONV_ROWS
        acc = None
        for k in range(SC_WIDTH):
            tap = shifted(base + k - 1, 0, D_SC) * scw_ref[k:k + 1, :]
            acc = tap if acc is None else acc + tap
        y_sc = sb_ref[rc * CONV_ROWS:(rc + 1) * CONV_ROWS, :] * acc
        ycat_ref[rc * CONV_ROWS:(rc + 1) * CONV_ROWS, 0:D_SC] = y_sc.astype(BF16)
        acc = None
        for k in range(CC_WIDTH):
            tap = shifted(base + k - CC_PAD, D_SC, D_SC + D_CC) * ccw_ref[k:k + 1, :]
            acc = tap if acc is None else acc + tap
        u = _layer_norm(acc + ccb, ccg, ccbeta)
        y_cc = u * jax.nn.sigmoid(u)
        ycat_ref[rc * CONV_ROWS:(rc + 1) * CONV_ROWS, D_SC + D_ATT:] = y_cc.astype(BF16)
        return _fold_to_vreg(y_sc) + _fold_to_vreg(y_cc)

    qi = lax.broadcasted_iota(jnp.int32, (BLOCK, BLOCK), 0)
    kj = lax.broadcasted_iota(jnp.int32, (BLOCK, BLOCK), 1)
    band_prev = kj >= qi
    band_next = kj <= qi
    low_pair = lax.broadcasted_iota(jnp.int32, (BLOCK, LANES), 1) < HEAD_DIM
    contract_lanes = (((1,), (1,)), ((), ()))

    def attention(j, kvh, anchor):
        valid_prev = band_prev & jnp.logical_not(first) if j == 0 else band_prev
        valid_next = band_next & jnp.logical_not(last) if j == n_blocks - 1 else band_next
        if anchor is None:
            masked = jnp.full((BLOCK, BLOCK), MASK_VALUE, F32)
        else:
            masked = jnp.tile(MASK_VALUE + _zeros_from(anchor), (BLOCK // SUBLANES, 1))
        q_rows = slice(j * BLOCK, (j + 1) * BLOCK)
        k_rows = slice(j * BLOCK, (j + 3) * BLOCK)
        qp = jnp.concatenate(
            [q_ref[q_rows, (2 * kvh) * LANES:(2 * kvh + 1) * LANES],
             q_ref[q_rows, (2 * kvh + 1) * LANES:(2 * kvh + 2) * LANES]], axis=0)
        probs = []
        rinv = []
        for parity in range(2):
            k_blk = kvx_ref[k_rows, (2 * kvh + parity) * LANES:(2 * kvh + parity + 1) * LANES]
            s_all = lax.dot_general(qp, k_blk, contract_lanes, preferred_element_type=F32)
            p_slabs = []
            for pair in range(2):
                h = GQA_GROUP * kvh + 2 * pair + parity
                rows = slice(pair * BLOCK, (pair + 1) * BLOCK)
                s_prev = jnp.where(valid_prev, s_all[rows, 0:BLOCK], masked)
                s_own = s_all[rows, BLOCK:2 * BLOCK]
                s_next = jnp.where(valid_next, s_all[rows, 2 * BLOCK:3 * BLOCK], masked)
                sink = sink_ref[h] * LOG2_E
                row_max = jnp.max(jnp.maximum(jnp.maximum(s_prev, s_own), s_next), axis=-1, keepdims=True)
                m = jnp.maximum(row_max, sink)
                p_prev = jnp.exp2(s_prev - m)
                p_own = jnp.exp2(s_own - m)
                p_next = jnp.exp2(s_next - m)
                denom = jnp.sum(p_prev + p_own + p_next, axis=-1, keepdims=True) + jnp.exp2(sink - m)
                rinv.append(1.0 / denom)
                p_slabs.append(jnp.concatenate(
                    [p_prev.astype(BF16), p_own.astype(BF16), p_next.astype(BF16)], axis=1))
            probs.append(jnp.concatenate(p_slabs, axis=0))
        v_off = 2 * N_KV_HEADS * LANES + 2 * kvh * LANES
        o = (jnp.dot(probs[0], kvx_ref[k_rows, v_off:v_off + LANES], preferred_element_type=F32)
             + jnp.dot(probs[1], kvx_ref[k_rows, v_off + LANES:v_off + 2 * LANES],
                       preferred_element_type=F32))
        for pair in range(2):
            scale = jnp.where(low_pair, rinv[pair], rinv[2 + pair])
            col = D_SC + (2 * kvh + pair) * LANES
            ycat_ref[q_rows, col:col + LANES] = (
                o[pair * BLOCK:(pair + 1) * BLOCK, :] * scale).astype(BF16)
        return _fold_to_vreg(o)

    n_steps = n_blocks * N_KV_HEADS
    assert MIX_ROWS // CONV_ROWS == n_steps
    anchors = []
    for step in range(n_steps):
        anchor = anchors[step - ATT_LAG] if step >= ATT_LAG else None
        done = attention(step // N_KV_HEADS, step % N_KV_HEADS, anchor)
        anchors.append(done + conv_chunk(step))

    mixed = jnp.dot(ycat_ref[...], wout_ref[...], preferred_element_type=F32)
    o_ref[...] = _layer_norm(ALPHA * x_ref[...] + mixed, g_ref[...], b_ref[...])


def _mixer(x, sb, cv, q, kv, sink, scw, ccw, ccb, ccg, ccbeta, wout, g, b, seq, layer):
    rows = x.shape[0]
    tiles_per_seq = seq // MIX_ROWS
    n_tiles = rows // MIX_ROWS
    halo_per_tile = MIX_ROWS // HALO_ROWS
    blocks_per_tile = MIX_ROWS // BLOCK
    row_spec = lambda width: pl.BlockSpec((MIX_ROWS, width), lambda i: (i, 0))
    cv_prev = pl.BlockSpec((HALO_ROWS, D_SC + D_CC),
                           lambda i: (jnp.maximum(i * halo_per_tile - 1, 0), 0))
    cv_next = pl.BlockSpec((HALO_ROWS, D_SC + D_CC),
                           lambda i: (jnp.minimum((i + 1) * halo_per_tile, n_tiles * halo_per_tile - 1), 0))
    kv_prev = pl.BlockSpec((BLOCK, KV_WIDTH),
                           lambda i: (jnp.maximum(i * blocks_per_tile - 1, 0), 0))
    kv_next = pl.BlockSpec((BLOCK, KV_WIDTH),
                           lambda i: (jnp.minimum((i + 1) * blocks_per_tile, n_tiles * blocks_per_tile - 1), 0))
    ext_rows = MIX_ROWS + 2 * HALO_ROWS
    return pl.pallas_call(
        functools.partial(_mixer_body, tiles_per_seq=tiles_per_seq),
        grid=(n_tiles,),
        in_specs=[
            pl.BlockSpec(memory_space=pltpu.SMEM),
            row_spec(D_MODEL), row_spec(D_SC),
            cv_prev, row_spec(D_SC + D_CC), cv_next,
            row_spec(D_ATT),
            kv_prev, row_spec(KV_WIDTH), kv_next,
            _const_spec((SC_WIDTH, D_SC)), _const_spec((CC_WIDTH, D_CC)),
            _const_spec((1, D_CC)), _const_spec((1, D_CC)), _const_spec((1, D_CC)),
            _layer_weight_spec(layer, (D_MODEL, D_MODEL)),
            _const_spec((1, D_MODEL)), _const_spec((1, D_MODEL)),
        ],
        out_specs=row_spec(D_MODEL),
        out_shape=jax.ShapeDtypeStruct((rows, D_MODEL), F32),
        scratch_shapes=[
            pltpu.VMEM((ext_rows, D_SC + D_CC), F32),
            pltpu.VMEM((SUBLANES - 1, ext_rows, D_SC + D_CC), F32),
            pltpu.VMEM((MIX_ROWS + 2 * BLOCK, 2 * KV_WIDTH), BF16),
            pltpu.VMEM((MIX_ROWS, D_MODEL), BF16),
        ],
        compiler_params=pltpu.CompilerParams(
            dimension_semantics=("arbitrary",), vmem_limit_bytes=VMEM_LIMIT_BYTES),
        name="mixer",
    )(sink, x, sb, cv, cv, cv, q, kv, kv, kv, scw, ccw, ccb, ccg, ccbeta, wout, g, b)


def _rope_tables(seq):
    half = HEAD_DIM // 2
    inv_freq = ROPE_THETA ** (-jnp.arange(half, dtype=F32) / half)
    ang = jnp.arange(seq).astype(F32)[:, None] * inv_freq[None, :]
    cos = jnp.cos(ang)
    sin = jnp.sin(ang)
    reps = LANES // HEAD_DIM
    cos_t = jnp.tile(jnp.concatenate([cos, cos], axis=-1), (1, reps))
    sin_t = jnp.tile(jnp.concatenate([-sin, sin], axis=-1), (1, reps))
    return cos_t, sin_t


def kernel(x, ffn1_w_gu, ffn1_w_down, ln1_g, ln1_b, w_in, sc_conv_w, attn_sink, cc_conv_w, cc_conv_b,
           cc_ln_g, cc_ln_b, w_out, ln2_g, ln2_b, ffn2_w_gu, ffn2_w_down, ln3_g, ln3_b):
    batch, seq, d = x.shape
    assert d == D_MODEL and seq % MIX_ROWS == 0 and seq % FFN_ROWS == 0
    cos_t, sin_t = _rope_tables(seq)
    row = lambda v: v.reshape(1, -1)
    h = x.reshape(batch * seq, d)
    wgu1, wd1, wgu2, wd2 = (w.astype(BF16) for w in (ffn1_w_gu, ffn1_w_down, ffn2_w_gu, ffn2_w_down))
    win, wout = w_in.astype(BF16), w_out.astype(BF16)
    for l in range(DEPTH):
        h = _ffn(h, wgu1, wd1, row(ln1_g[l]), row(ln1_b[l]), l)
        sb, cv, q, kv = _inproj(h, win, cos_t, sin_t, seq, l)
        h = _mixer(h, sb, cv, q, kv, attn_sink[l], sc_conv_w[l], cc_conv_w[l], row(cc_conv_b[l]),
                   row(cc_ln_g[l]), row(cc_ln_b[l]), wout, row(ln2_g[l]), row(ln2_b[l]), seq, l)
        h = _ffn(h, wgu2, wd2, row(ln3_g[l]), row(ln3_b[l]), l)
    return h.reshape(batch, seq, d)
```

```python
import functools

import jax
import jax.numpy as jnp
from jax import lax
from jax.experimental import pallas as pl
from jax.experimental.pallas import tpu as pltpu

D_MODEL = 1024
DEPTH = 2
HEAD_DIM = 64
D_SC = 256
D_ATT = 512
D_CC = 256
N_Q_HEADS = 8
N_KV_HEADS = 2
GQA_GROUP = N_Q_HEADS // N_KV_HEADS
SC_WIDTH = 3
CC_WIDTH = 31
CC_PAD = (CC_WIDTH - 1) // 2
WINDOW = 128
BLOCK = 128
ROPE_THETA = 10000.0
D_FF = 2816
LN_EPS = 1e-5
ALPHA = (2.0 * DEPTH) ** 0.25
D_IN = 3 * D_SC + D_ATT + 2 * N_KV_HEADS * HEAD_DIM + 2 * D_CC
MASK_VALUE = -1e30

OFF_SC_B = 0
OFF_SC_C = D_SC
OFF_SC_H = 2 * D_SC
OFF_Q = 3 * D_SC
OFF_K = OFF_Q + D_ATT
OFF_V = OFF_K + N_KV_HEADS * HEAD_DIM
OFF_CC_A = OFF_V + N_KV_HEADS * HEAD_DIM
OFF_CC_G = OFF_CC_A + D_CC

LANES = 128
SUBLANES = 8
HALO_ROWS = 16
VMEM_LIMIT_BYTES = 56 * 1024 * 1024

FFN_ROWS = 512
INPROJ_ROWS = 512
FFN_CHUNK = 256
N_CHUNKS = D_FF // FFN_CHUNK
OUT_NORM_CHUNKS = (1, 2, 3, 4)
PRE_NORM_CHUNKS = (5, 6, 7, 8)
MIX_ROWS = 512
CONV_ROWS = 64
KV_WIDTH = 2 * N_KV_HEADS * LANES
ATT_LAG = 2
LOG2_E = 1.4426950408889634

F32 = jnp.float32
BF16 = jnp.bfloat16


def _layer_norm(y, g, b):
    mu = jnp.mean(y, axis=-1, keepdims=True)
    yc = y - mu
    var = jnp.mean(yc * yc, axis=-1, keepdims=True)
    return yc * lax.rsqrt(var + LN_EPS) * g + b


def _const_spec(shape):
    return pl.BlockSpec(shape, lambda i: (0,) * len(shape))


def _layer_weight_spec(layer, shape):
    return pl.BlockSpec((None,) + shape, lambda i: (layer,) + (0,) * len(shape),
                        pipeline_mode=pl.Buffered(1))


def _zeros_from(anchor):
    bits = lax.bitcast_convert_type(anchor, jnp.uint32)
    zero = lax.shift_right_logical(lax.shift_right_logical(bits, jnp.uint32(16)), jnp.uint32(16))
    return lax.bitcast_convert_type(zero, F32)


def _fold_to_vreg(v):
    acc = v[:, 0:LANES]
    for c in range(1, v.shape[1] // LANES):
        acc = acc + v[:, c * LANES:(c + 1) * LANES]
    out = acc[0:SUBLANES]
    for r in range(1, v.shape[0] // SUBLANES):
        out = out + acc[r * SUBLANES:(r + 1) * SUBLANES]
    return out


def _lane_block_sum(v):
    acc = v[:, 0:LANES]
    for c in range(1, v.shape[1] // LANES):
        acc = acc + v[:, c * LANES:(c + 1) * LANES]
    return acc


def _swiglu_residual(x, wgu_ref, wd_ref, anchors=None):
    xb = x.astype(BF16)
    acc = None
    for c in range(N_CHUNKS):
        g = jnp.dot(xb, wgu_ref[:, c * FFN_CHUNK:(c + 1) * FFN_CHUNK], preferred_element_type=F32)
        u = jnp.dot(xb, wgu_ref[:, D_FF + c * FFN_CHUNK:D_FF + (c + 1) * FFN_CHUNK],
                    preferred_element_type=F32)
        h = g * jax.nn.sigmoid(g) * u
        if anchors and c in anchors:
            zeros = _zeros_from(anchors[c])
            h = h + jnp.tile(zeros, (h.shape[0] // zeros.shape[0], FFN_CHUNK // LANES))
        d = jnp.dot(h.astype(BF16), wd_ref[c * FFN_CHUNK:(c + 1) * FFN_CHUNK, :],
                    preferred_element_type=F32)
        acc = d if acc is None else acc + d
    return ALPHA * x + 0.5 * acc


def _in_row_parts(compute, dst_ref, chunks):
    part = FFN_ROWS // len(chunks)
    anchors = {}
    for p, c in enumerate(chunks):
        rows = slice(p * part, (p + 1) * part)
        value = compute(rows)
        dst_ref[rows, :] = value
        anchors[c] = _lane_block_sum(value)
    return anchors


def _ffn_body(x_ref, wgu_ref, wd_ref, g_ref, b_ref, o_ref, y_ref):
    i = pl.program_id(0)
    n_tiles = pl.num_programs(0) - 1
    out_norm = lambda rows: _layer_norm(y_ref[rows, :], g_ref[...], b_ref[...])

    @pl.when(i == 0)
    def _():
        y_ref[...] = _swiglu_residual(x_ref[...], wgu_ref, wd_ref)

    @pl.when((i > 0) & (i < n_tiles))
    def _():
        anchors = _in_row_parts(out_norm, o_ref, OUT_NORM_CHUNKS)
        y_ref[...] = _swiglu_residual(x_ref[...], wgu_ref, wd_ref, anchors)

    @pl.when(i == n_tiles)
    def _():
        o_ref[...] = out_norm(slice(None))


def _ffn_prenorm_body(x_ref, m_ref, pg_ref, pb_ref, wgu_ref, wd_ref, g_ref, b_ref, o_ref, xn_ref, y_ref):
    j = pl.program_id(0)
    n_tiles = pl.num_programs(0) - 2
    out_norm = lambda rows: _layer_norm(y_ref[rows, :], g_ref[...], b_ref[...])
    pre_norm = lambda rows: _layer_norm(ALPHA * x_ref[rows, :] + m_ref[rows, :], pg_ref[...], pb_ref[...])

    @pl.when(j == 0)
    def _():
        xn_ref[0] = pre_norm(slice(None))
        y_ref[...] = jnp.zeros((FFN_ROWS, D_MODEL), F32)

    @pl.when((j > 0) & (j <= n_tiles))
    def _():
        x_cur = xn_ref[(j - 1) % 2]
        anchors = _in_row_parts(out_norm, o_ref, OUT_NORM_CHUNKS)
        anchors.update(_in_row_parts(pre_norm, xn_ref.at[j % 2], PRE_NORM_CHUNKS))
        y_ref[...] = _swiglu_residual(x_cur, wgu_ref, wd_ref, anchors)

    @pl.when(j == n_tiles + 1)
    def _():
        o_ref[...] = out_norm(slice(None))


def _ffn(x, wgu, wd, g, b, layer):
    rows = x.shape[0]
    n_tiles = rows // FFN_ROWS
    return pl.pallas_call(
        _ffn_body,
        grid=(n_tiles + 1,),
        in_specs=[
            pl.BlockSpec((FFN_ROWS, D_MODEL), lambda i: (jnp.minimum(i, n_tiles - 1), 0)),
            _layer_weight_spec(layer, (D_MODEL, 2 * D_FF)),
            _layer_weight_spec(layer, (D_FF, D_MODEL)),
            _const_spec((1, D_MODEL)),
            _const_spec((1, D_MODEL)),
        ],
        out_specs=pl.BlockSpec((FFN_ROWS, D_MODEL), lambda i: (jnp.maximum(i - 1, 0), 0)),
        out_shape=jax.ShapeDtypeStruct((rows, D_MODEL), F32),
        scratch_shapes=[pltpu.VMEM((FFN_ROWS, D_MODEL), F32)],
        compiler_params=pltpu.CompilerParams(
            dimension_semantics=("arbitrary",), vmem_limit_bytes=VMEM_LIMIT_BYTES),
        name="ffn",
    )(x, wgu, wd, g, b)


def _ffn_prenorm(x, mixed, pre_g, pre_b, wgu, wd, g, b, layer):
    rows = x.shape[0]
    n_tiles = rows // FFN_ROWS
    in_tile = pl.BlockSpec((FFN_ROWS, D_MODEL), lambda j: (jnp.minimum(j, n_tiles - 1), 0))
    return pl.pallas_call(
        _ffn_prenorm_body,
        grid=(n_tiles + 2,),
        in_specs=[
            in_tile, in_tile,
            _const_spec((1, D_MODEL)), _const_spec((1, D_MODEL)),
            _layer_weight_spec(layer, (D_MODEL, 2 * D_FF)),
            _layer_weight_spec(layer, (D_FF, D_MODEL)),
            _const_spec((1, D_MODEL)), _const_spec((1, D_MODEL)),
        ],
        out_specs=pl.BlockSpec((FFN_ROWS, D_MODEL), lambda j: (jnp.clip(j - 2, 0, n_tiles - 1), 0)),
        out_shape=jax.ShapeDtypeStruct((rows, D_MODEL), F32),
        scratch_shapes=[pltpu.VMEM((2, FFN_ROWS, D_MODEL), F32), pltpu.VMEM((FFN_ROWS, D_MODEL), F32)],
        compiler_params=pltpu.CompilerParams(
            dimension_semantics=("arbitrary",), vmem_limit_bytes=VMEM_LIMIT_BYTES),
        name="ffn_prenorm",
    )(x, mixed, pre_g, pre_b, wgu, wd, g, b)


def _inproj_body(x_ref, w_ref, cos_ref, sin_ref, sb_ref, cv_ref, q_ref, kv_ref):
    xb = x_ref[...].astype(BF16)
    z = jnp.dot(xb, w_ref[...], preferred_element_type=F32)
    sb_ref[...] = z[:, OFF_SC_B:OFF_SC_B + D_SC]
    cv_ref[:, 0:D_SC] = z[:, OFF_SC_C:OFF_SC_C + D_SC] * z[:, OFF_SC_H:OFF_SC_H + D_SC]
    cv_ref[:, D_SC:D_SC + D_CC] = (z[:, OFF_CC_A:OFF_CC_A + D_CC]
                                   * jax.nn.sigmoid(z[:, OFF_CC_G:OFF_CC_G + D_CC]))
    cos = cos_ref[...]
    sin = sin_ref[...]
    lane = lax.broadcasted_iota(jnp.int32, cos.shape, 1)
    first_half = (lane % HEAD_DIM) < (HEAD_DIM // 2)
    low_head = lane < HEAD_DIM

    def rope(t):
        rot = jnp.where(first_half, pltpu.roll(t, LANES - HEAD_DIM // 2, 1),
                        pltpu.roll(t, HEAD_DIM // 2, 1))
        return t * cos + rot * sin

    def duplicate_heads(t, o_ref, off):
        swapped = pltpu.roll(t, HEAD_DIM, 1)
        o_ref[:, off:off + LANES] = jnp.where(low_head, t, swapped).astype(BF16)
        o_ref[:, off + LANES:off + 2 * LANES] = jnp.where(low_head, swapped, t).astype(BF16)

    for c in range(D_ATT // LANES):
        t = z[:, OFF_Q + c * LANES:OFF_Q + (c + 1) * LANES]
        q_ref[:, c * LANES:(c + 1) * LANES] = (rope(t) * (HEAD_DIM ** -0.5 * LOG2_E)).astype(BF16)
    duplicate_heads(rope(z[:, OFF_K:OFF_K + LANES]), kv_ref, 0)
    duplicate_heads(z[:, OFF_V:OFF_V + LANES], kv_ref, N_KV_HEADS * LANES)


def _inproj(x, w, cos, sin, seq, layer):
    rows = x.shape[0]
    tiles_per_seq = seq // INPROJ_ROWS
    row_spec = lambda width: pl.BlockSpec((INPROJ_ROWS, width), lambda i: (i, 0))
    pos_spec = pl.BlockSpec((INPROJ_ROWS, LANES), lambda i: (i % tiles_per_seq, 0))
    return pl.pallas_call(
        _inproj_body,
        grid=(rows // INPROJ_ROWS,),
        in_specs=[row_spec(D_MODEL), _layer_weight_spec(layer, (D_MODEL, D_IN)), pos_spec, pos_spec],
        out_specs=[row_spec(D_SC), row_spec(D_SC + D_CC), row_spec(D_ATT), row_spec(KV_WIDTH)],
        out_shape=[
            jax.ShapeDtypeStruct((rows, D_SC), F32),
            jax.ShapeDtypeStruct((rows, D_SC + D_CC), F32),
            jax.ShapeDtypeStruct((rows, D_ATT), BF16),
            jax.ShapeDtypeStruct((rows, KV_WIDTH), BF16),
        ],
        compiler_params=pltpu.CompilerParams(
            dimension_semantics=("arbitrary",), vmem_limit_bytes=VMEM_LIMIT_BYTES),
        name="inproj",
    )(x, w, cos, sin)


def _mixer_body(sink_ref, sb_ref, cvp_ref, cv_ref, cvn_ref, q_ref, kvp_ref, kv_ref, kvn_ref,
                scw_ref, ccw_ref, ccb_ref, ccg_ref, ccbeta_ref, wout_ref,
                o_ref, cvx_ref, shf_ref, kvx_ref, ycat_ref, *, tiles_per_seq):
    it = pl.program_id(0) % tiles_per_seq
    first = it == 0
    last = it == tiles_per_seq - 1
    n_blocks = MIX_ROWS // BLOCK
    ext_rows = MIX_ROWS + 2 * HALO_ROWS

    cvx_ref[0:HALO_ROWS, :] = jnp.where(first, 0.0, cvp_ref[...])
    cvx_ref[HALO_ROWS:HALO_ROWS + MIX_ROWS, :] = cv_ref[...]
    cvx_ref[HALO_ROWS + MIX_ROWS:, :] = jnp.where(last, 0.0, cvn_ref[...])
    ext_u = cvx_ref[:, D_SC:]
    for r in range(1, SUBLANES):
        shf_ref[r - 1, :, D_SC:] = pltpu.roll(ext_u, ext_rows - r, 0)
    ext_p = cvx_ref[:, :D_SC]
    for r in sorted({(HALO_ROWS + k - 1) % SUBLANES for k in range(SC_WIDTH)} - {0}):
        shf_ref[r - 1, :, :D_SC] = pltpu.roll(ext_p, ext_rows - r, 0)

    def shifted(row, lo_lane, hi_lane):
        r = row % SUBLANES
        if r == 0:
            return cvx_ref[row:row + CONV_ROWS, lo_lane:hi_lane]
        return shf_ref[r - 1, row - r:row - r + CONV_ROWS, lo_lane:hi_lane]

    lane = lax.broadcasted_iota(jnp.int32, (BLOCK, LANES), 1)
    low = lane < HEAD_DIM
    for src_ref, row0, n in ((kvp_ref, 0, BLOCK), (kv_ref, BLOCK, MIX_ROWS),
                             (kvn_ref, BLOCK + MIX_ROWS, BLOCK)):
        for rb in range(n // BLOCK):
            for c in range(KV_WIDTH // LANES):
                blk = src_ref[rb * BLOCK:(rb + 1) * BLOCK, c * LANES:(c + 1) * LANES]
                rows = slice(row0 + rb * BLOCK, row0 + (rb + 1) * BLOCK)
                kvx_ref[rows, 2 * c * LANES:(2 * c + 1) * LANES] = jnp.where(low, blk, 0)
                kvx_ref[rows, (2 * c + 1) * LANES:(2 * c + 2) * LANES] = jnp.where(low, 0, blk)

    ccb = ccb_ref[...]
    ccg = ccg_ref[...]
    ccbeta = ccbeta_ref[...]

    def tap_weight(w_ref, k):
        return jnp.tile(w_ref[k], (CONV_ROWS // SUBLANES, 1))

    def conv_chunk(rc):
        base = HALO_ROWS + rc * CONV_ROWS
        acc = None
        for k in range(SC_WIDTH):
            tap = shifted(base + k - 1, 0, D_SC) * tap_weight(scw_ref, k)
            acc = tap if acc is None else acc + tap
        y_sc = sb_ref[rc * CONV_ROWS:(rc + 1) * CONV_ROWS, :] * acc
        ycat_ref[rc * CONV_ROWS:(rc + 1) * CONV_ROWS, 0:D_SC] = y_sc.astype(BF16)
        acc = None
        for k in range(CC_WIDTH):
            tap = shifted(base + k - CC_PAD, D_SC, D_SC + D_CC) * tap_weight(ccw_ref, k)
            acc = tap if acc is None else acc + tap
        u = _layer_norm(acc + ccb, ccg, ccbeta)
        y_cc = u * jax.nn.sigmoid(u)
        ycat_ref[rc * CONV_ROWS:(rc + 1) * CONV_ROWS, D_SC + D_ATT:] = y_cc.astype(BF16)
        return _fold_to_vreg(y_sc) + _fold_to_vreg(y_cc)

    qi = lax.broadcasted_iota(jnp.int32, (BLOCK, BLOCK), 0)
    kj = lax.broadcasted_iota(jnp.int32, (BLOCK, BLOCK), 1)
    band_prev = kj >= qi
    band_next = kj <= qi
    low_pair = lax.broadcasted_iota(jnp.int32, (BLOCK, LANES), 1) < HEAD_DIM
    contract_lanes = (((1,), (1,)), ((), ()))

    def attention(j, kvh, anchor):
        valid_prev = band_prev & jnp.logical_not(first) if j == 0 else band_prev
        valid_next = band_next & jnp.logical_not(last) if j == n_blocks - 1 else band_next
        if anchor is None:
            masked = jnp.full((BLOCK, BLOCK), MASK_VALUE, F32)
        else:
            masked = jnp.tile(MASK_VALUE + _zeros_from(anchor), (BLOCK // SUBLANES, 1))
        q_rows = slice(j * BLOCK, (j + 1) * BLOCK)
        k_rows = slice(j * BLOCK, (j + 3) * BLOCK)
        qp = jnp.concatenate(
            [q_ref[q_rows, (2 * kvh) * LANES:(2 * kvh + 1) * LANES],
             q_ref[q_rows, (2 * kvh + 1) * LANES:(2 * kvh + 2) * LANES]], axis=0)
        probs = []
        rinv = []
        for parity in range(2):
            k_blk = kvx_ref[k_rows, (2 * kvh + parity) * LANES:(2 * kvh + parity + 1) * LANES]
            s_all = lax.dot_general(qp, k_blk, contract_lanes, preferred_element_type=F32)
            p_slabs = []
            for pair in range(2):
                h = GQA_GROUP * kvh + 2 * pair + parity
                rows = slice(pair * BLOCK, (pair + 1) * BLOCK)
                s_prev = jnp.where(valid_prev, s_all[rows, 0:BLOCK], masked)
                s_own = s_all[rows, BLOCK:2 * BLOCK]
                s_next = jnp.where(valid_next, s_all[rows, 2 * BLOCK:3 * BLOCK], masked)
                sink = sink_ref[h] * LOG2_E
                row_max = jnp.max(jnp.maximum(jnp.maximum(s_prev, s_own), s_next), axis=-1, keepdims=True)
                m = jnp.maximum(row_max, sink)
                p_prev = jnp.exp2(s_prev - m)
                p_own = jnp.exp2(s_own - m)
                p_next = jnp.exp2(s_next - m)
                denom = jnp.sum(p_prev + p_own + p_next, axis=-1, keepdims=True) + jnp.exp2(sink - m)
                rinv.append(1.0 / denom)
                p_slabs.append(jnp.concatenate(
                    [p_prev.astype(BF16), p_own.astype(BF16), p_next.astype(BF16)], axis=1))
            probs.append(jnp.concatenate(p_slabs, axis=0))
        v_off = 2 * N_KV_HEADS * LANES + 2 * kvh * LANES
        o = (jnp.dot(probs[0], kvx_ref[k_rows, v_off:v_off + LANES], preferred_element_type=F32)
             + jnp.dot(probs[1], kvx_ref[k_rows, v_off + LANES:v_off + 2 * LANES],
                       preferred_element_type=F32))
        for pair in range(2):
            scale = jnp.where(low_pair, rinv[pair], rinv[2 + pair])
            col = D_SC + (2 * kvh + pair) * LANES
            ycat_ref[q_rows, col:col + LANES] = (
                o[pair * BLOCK:(pair + 1) * BLOCK, :] * scale).astype(BF16)
        return _fold_to_vreg(o)

    n_steps = n_blocks * N_KV_HEADS
    assert MIX_ROWS // CONV_ROWS == n_steps
    anchors = []
    for step in range(n_steps):
        anchor = anchors[step - ATT_LAG] if step >= ATT_LAG else None
        done = attention(step // N_KV_HEADS, step % N_KV_HEADS, anchor)
        anchors.append(done + conv_chunk(step))

    o_ref[...] = jnp.dot(ycat_ref[...], wout_ref[...], preferred_element_type=F32)


def _mixer(sb, cv, q, kv, sink, scw, ccw, ccb, ccg, ccbeta, wout, seq, layer):
    rows = sb.shape[0]
    tiles_per_seq = seq // MIX_ROWS
    n_tiles = rows // MIX_ROWS
    halo_per_tile = MIX_ROWS // HALO_ROWS
    blocks_per_tile = MIX_ROWS // BLOCK
    row_spec = lambda width: pl.BlockSpec((MIX_ROWS, width), lambda i: (i, 0))
    cv_prev = pl.BlockSpec((HALO_ROWS, D_SC + D_CC),
                           lambda i: (jnp.maximum(i * halo_per_tile - 1, 0), 0))
    cv_next = pl.BlockSpec((HALO_ROWS, D_SC + D_CC),
                           lambda i: (jnp.minimum((i + 1) * halo_per_tile, n_tiles * halo_per_tile - 1), 0))
    kv_prev = pl.BlockSpec((BLOCK, KV_WIDTH),
                           lambda i: (jnp.maximum(i * blocks_per_tile - 1, 0), 0))
    kv_next = pl.BlockSpec((BLOCK, KV_WIDTH),
                           lambda i: (jnp.minimum((i + 1) * blocks_per_tile, n_tiles * blocks_per_tile - 1), 0))
    ext_rows = MIX_ROWS + 2 * HALO_ROWS
    return pl.pallas_call(
        functools.partial(_mixer_body, tiles_per_seq=tiles_per_seq),
        grid=(n_tiles,),
        in_specs=[
            pl.BlockSpec(memory_space=pltpu.SMEM),
            row_spec(D_SC),
            cv_prev, row_spec(D_SC + D_CC), cv_next,
            row_spec(D_ATT),
            kv_prev, row_spec(KV_WIDTH), kv_next,
            _const_spec((SC_WIDTH, SUBLANES, D_SC)), _const_spec((CC_WIDTH, SUBLANES, D_CC)),
            _const_spec((1, D_CC)), _const_spec((1, D_CC)), _const_spec((1, D_CC)),
            _layer_weight_spec(layer, (D_MODEL, D_MODEL)),
        ],
        out_specs=row_spec(D_MODEL),
        out_shape=jax.ShapeDtypeStruct((rows, D_MODEL), F32),
        scratch_shapes=[
            pltpu.VMEM((ext_rows, D_SC + D_CC), F32),
            pltpu.VMEM((SUBLANES - 1, ext_rows, D_SC + D_CC), F32),
            pltpu.VMEM((MIX_ROWS + 2 * BLOCK, 2 * KV_WIDTH), BF16),
            pltpu.VMEM((MIX_ROWS, D_MODEL), BF16),
        ],
        compiler_params=pltpu.CompilerParams(
            dimension_semantics=("arbitrary",), vmem_limit_bytes=VMEM_LIMIT_BYTES),
        name="mixer",
    )(sink, sb, cv, cv, cv, q, kv, kv, kv, scw, ccw, ccb, ccg, ccbeta, wout)


def _rope_tables(seq):
    half = HEAD_DIM // 2
    inv_freq = ROPE_THETA ** (-jnp.arange(half, dtype=F32) / half)
    ang = jnp.arange(seq).astype(F32)[:, None] * inv_freq[None, :]
    cos = jnp.cos(ang)
    sin = jnp.sin(ang)
    reps = LANES // HEAD_DIM
    cos_t = jnp.tile(jnp.concatenate([cos, cos], axis=-1), (1, reps))
    sin_t = jnp.tile(jnp.concatenate([-sin, sin], axis=-1), (1, reps))
    return cos_t, sin_t


def kernel(x, ffn1_w_gu, ffn1_w_down, ln1_g, ln1_b, w_in, sc_conv_w, attn_sink, cc_conv_w, cc_conv_b,
           cc_ln_g, cc_ln_b, w_out, ln2_g, ln2_b, ffn2_w_gu, ffn2_w_down, ln3_g, ln3_b):
    batch, seq, d = x.shape
    assert d == D_MODEL and seq % MIX_ROWS == 0 and seq % INPROJ_ROWS == 0
    cos_t, sin_t = _rope_tables(seq)
    row = lambda v: v.reshape(1, -1)
    h = x.reshape(batch * seq, d)
    wgu1, wd1, wgu2, wd2 = (w.astype(BF16) for w in (ffn1_w_gu, ffn1_w_down, ffn2_w_gu, ffn2_w_down))
    win, wout = w_in.astype(BF16), w_out.astype(BF16)
    taps = lambda w: jnp.broadcast_to(w[:, None, :], (w.shape[0], SUBLANES, w.shape[1]))
    for l in range(DEPTH):
        h = _ffn(h, wgu1, wd1, row(ln1_g[l]), row(ln1_b[l]), l)
        sb, cv, q, kv = _inproj(h, win, cos_t, sin_t, seq, l)
        mixed = _mixer(sb, cv, q, kv, attn_sink[l], taps(sc_conv_w[l]), taps(cc_conv_w[l]),
                       row(cc_conv_b[l]), row(cc_ln_g[l]), row(cc_ln_b[l]), wout, seq, l)
        h = _ffn_prenorm(h, mixed, row(ln2_g[l]), row(ln2_b[l]), wgu2, wd2,
                         row(ln3_g[l]), row(ln3_b[l]), l)
    return h.reshape(batch, seq, d)
```

```python
import functools

import jax
import jax.numpy as jnp
from jax import lax
from jax.experimental import pallas as pl
from jax.experimental.pallas import tpu as pltpu

D_MODEL = 1024
DEPTH = 2
HEAD_DIM = 64
D_SC = 256
D_ATT = 512
D_CC = 256
N_Q_HEADS = 8
N_KV_HEADS = 2
GQA_GROUP = N_Q_HEADS // N_KV_HEADS
SC_WIDTH = 3
CC_WIDTH = 31
CC_PAD = (CC_WIDTH - 1) // 2
WINDOW = 128
BLOCK = 128
ROPE_THETA = 10000.0
D_FF = 2816
LN_EPS = 1e-5
ALPHA = (2.0 * DEPTH) ** 0.25
D_IN = 3 * D_SC + D_ATT + 2 * N_KV_HEADS * HEAD_DIM + 2 * D_CC
MASK_VALUE = -1e30

OFF_SC_B = 0
OFF_SC_C = D_SC
OFF_SC_H = 2 * D_SC
OFF_Q = 3 * D_SC
OFF_K = OFF_Q + D_ATT
OFF_V = OFF_K + N_KV_HEADS * HEAD_DIM
OFF_CC_A = OFF_V + N_KV_HEADS * HEAD_DIM
OFF_CC_G = OFF_CC_A + D_CC

LANES = 128
SUBLANES = 8
HALO_ROWS = 16
VMEM_LIMIT_BYTES = 56 * 1024 * 1024

FFN_ROWS = 512
INPROJ_ROWS = 1024
FFN_CHUNK = 256
N_CHUNKS = D_FF // FFN_CHUNK
OUT_NORM_CHUNKS = (1, 2, 3, 4)
PRE_NORM_CHUNKS = (5, 6, 7, 8)
MIX_ROWS = 1024
CONV_ROWS = 64
KV_WIDTH = 2 * N_KV_HEADS * LANES
ATT_LAG = 2
LOG2_E = 1.4426950408889634

F32 = jnp.float32
BF16 = jnp.bfloat16


def _layer_norm(y, g, b):
    mu = jnp.mean(y, axis=-1, keepdims=True)
    yc = y - mu
    var = jnp.mean(yc * yc, axis=-1, keepdims=True)
    return yc * lax.rsqrt(var + LN_EPS) * g + b


def _const_spec(shape):
    return pl.BlockSpec(shape, lambda i: (0,) * len(shape))


def _layer_weight_spec(layer, shape):
    return pl.BlockSpec((None,) + shape, lambda i: (layer,) + (0,) * len(shape),
                        pipeline_mode=pl.Buffered(1))


def _zeros_from(anchor):
    bits = lax.bitcast_convert_type(anchor, jnp.uint32)
    zero = lax.shift_right_logical(lax.shift_right_logical(bits, jnp.uint32(16)), jnp.uint32(16))
    return lax.bitcast_convert_type(zero, F32)


def _fold_to_vreg(v):
    acc = v[:, 0:LANES]
    for c in range(1, v.shape[1] // LANES):
        acc = acc + v[:, c * LANES:(c + 1) * LANES]
    out = acc[0:SUBLANES]
    for r in range(1, v.shape[0] // SUBLANES):
        out = out + acc[r * SUBLANES:(r + 1) * SUBLANES]
    return out


def _lane_block_sum(v):
    acc = v[:, 0:LANES]
    for c in range(1, v.shape[1] // LANES):
        acc = acc + v[:, c * LANES:(c + 1) * LANES]
    return acc


def _swiglu_residual(x, wgu_ref, wd_ref, anchors=None):
    xb = x.astype(BF16)
    acc = None
    for c in range(N_CHUNKS):
        g = jnp.dot(xb, wgu_ref[:, c * FFN_CHUNK:(c + 1) * FFN_CHUNK], preferred_element_type=F32)
        u = jnp.dot(xb, wgu_ref[:, D_FF + c * FFN_CHUNK:D_FF + (c + 1) * FFN_CHUNK],
                    preferred_element_type=F32)
        h = g * jax.nn.sigmoid(g) * u
        if anchors and c in anchors:
            zeros = _zeros_from(anchors[c])
            h = h + jnp.tile(zeros, (h.shape[0] // zeros.shape[0], FFN_CHUNK // LANES))
        d = jnp.dot(h.astype(BF16), wd_ref[c * FFN_CHUNK:(c + 1) * FFN_CHUNK, :],
                    preferred_element_type=F32)
        acc = d if acc is None else acc + d
    return ALPHA * x + 0.5 * acc


def _in_row_parts(compute, dst_ref, chunks):
    part = FFN_ROWS // len(chunks)
    anchors = {}
    for p, c in enumerate(chunks):
        rows = slice(p * part, (p + 1) * part)
        value = compute(rows)
        dst_ref[rows, :] = value
        anchors[c] = _lane_block_sum(value)
    return anchors


def _ffn_body(x_ref, wgu_ref, wd_ref, g_ref, b_ref, o_ref, y_ref):
    i = pl.program_id(0)
    n_tiles = pl.num_programs(0) - 1
    out_norm = lambda rows: _layer_norm(y_ref[rows, :], g_ref[...], b_ref[...])

    @pl.when(i == 0)
    def _():
        y_ref[...] = _swiglu_residual(x_ref[...], wgu_ref, wd_ref)

    @pl.when((i > 0) & (i < n_tiles))
    def _():
        anchors = _in_row_parts(out_norm, o_ref, OUT_NORM_CHUNKS)
        y_ref[...] = _swiglu_residual(x_ref[...], wgu_ref, wd_ref, anchors)

    @pl.when(i == n_tiles)
    def _():
        o_ref[...] = out_norm(slice(None))


def _ffn_prenorm_body(x_ref, m_ref, pg_ref, pb_ref, wgu_ref, wd_ref, g_ref, b_ref, o_ref, xn_ref, y_ref):
    j = pl.program_id(0)
    n_tiles = pl.num_programs(0) - 2
    out_norm = lambda rows: _layer_norm(y_ref[rows, :], g_ref[...], b_ref[...])
    pre_norm = lambda rows: _layer_norm(ALPHA * x_ref[rows, :] + m_ref[rows, :], pg_ref[...], pb_ref[...])

    @pl.when(j == 0)
    def _():
        xn_ref[0] = pre_norm(slice(None))
        y_ref[...] = jnp.zeros((FFN_ROWS, D_MODEL), F32)

    @pl.when((j > 0) & (j <= n_tiles))
    def _():
        x_cur = xn_ref[(j - 1) % 2]
        anchors = _in_row_parts(out_norm, o_ref, OUT_NORM_CHUNKS)
        anchors.update(_in_row_parts(pre_norm, xn_ref.at[j % 2], PRE_NORM_CHUNKS))
        y_ref[...] = _swiglu_residual(x_cur, wgu_ref, wd_ref, anchors)

    @pl.when(j == n_tiles + 1)
    def _():
        o_ref[...] = out_norm(slice(None))


def _ffn(x, wgu, wd, g, b, layer):
    rows = x.shape[0]
    n_tiles = rows // FFN_ROWS
    return pl.pallas_call(
        _ffn_body,
        grid=(n_tiles + 1,),
        in_specs=[
            pl.BlockSpec((FFN_ROWS, D_MODEL), lambda i: (jnp.minimum(i, n_tiles - 1), 0)),
            _layer_weight_spec(layer, (D_MODEL, 2 * D_FF)),
            _layer_weight_spec(layer, (D_FF, D_MODEL)),
            _const_spec((1, D_MODEL)),
            _const_spec((1, D_MODEL)),
        ],
        out_specs=pl.BlockSpec((FFN_ROWS, D_MODEL), lambda i: (jnp.maximum(i - 1, 0), 0)),
        out_shape=jax.ShapeDtypeStruct((rows, D_MODEL), F32),
        scratch_shapes=[pltpu.VMEM((FFN_ROWS, D_MODEL), F32)],
        compiler_params=pltpu.CompilerParams(
            dimension_semantics=("arbitrary",), vmem_limit_bytes=VMEM_LIMIT_BYTES),
        name="ffn",
    )(x, wgu, wd, g, b)


def _ffn_prenorm(x, mixed, pre_g, pre_b, wgu, wd, g, b, layer):
    rows = x.shape[0]
    n_tiles = rows // FFN_ROWS
    in_tile = pl.BlockSpec((FFN_ROWS, D_MODEL), lambda j: (jnp.minimum(j, n_tiles - 1), 0))
    return pl.pallas_call(
        _ffn_prenorm_body,
        grid=(n_tiles + 2,),
        in_specs=[
            in_tile, in_tile,
            _const_spec((1, D_MODEL)), _const_spec((1, D_MODEL)),
            _layer_weight_spec(layer, (D_MODEL, 2 * D_FF)),
            _layer_weight_spec(layer, (D_FF, D_MODEL)),
            _const_spec((1, D_MODEL)), _const_spec((1, D_MODEL)),
        ],
        out_specs=pl.BlockSpec((FFN_ROWS, D_MODEL), lambda j: (jnp.clip(j - 2, 0, n_tiles - 1), 0)),
        out_shape=jax.ShapeDtypeStruct((rows, D_MODEL), F32),
        scratch_shapes=[pltpu.VMEM((2, FFN_ROWS, D_MODEL), F32), pltpu.VMEM((FFN_ROWS, D_MODEL), F32)],
        compiler_params=pltpu.CompilerParams(
            dimension_semantics=("arbitrary",), vmem_limit_bytes=VMEM_LIMIT_BYTES),
        name="ffn_prenorm",
    )(x, mixed, pre_g, pre_b, wgu, wd, g, b)


def _inproj_body(x_ref, w_ref, cos_ref, sin_ref, sb_ref, cv_ref, q_ref, kv_ref):
    xb = x_ref[...].astype(BF16)
    z = jnp.dot(xb, w_ref[...], preferred_element_type=F32)
    sb_ref[...] = z[:, OFF_SC_B:OFF_SC_B + D_SC]
    cv_ref[:, 0:D_SC] = z[:, OFF_SC_C:OFF_SC_C + D_SC] * z[:, OFF_SC_H:OFF_SC_H + D_SC]
    cv_ref[:, D_SC:D_SC + D_CC] = (z[:, OFF_CC_A:OFF_CC_A + D_CC]
                                   * jax.nn.sigmoid(z[:, OFF_CC_G:OFF_CC_G + D_CC]))
    cos = cos_ref[...]
    sin = sin_ref[...]
    lane = lax.broadcasted_iota(jnp.int32, cos.shape, 1)
    first_half = (lane % HEAD_DIM) < (HEAD_DIM // 2)
    low_head = lane < HEAD_DIM

    def rope(t):
        rot = jnp.where(first_half, pltpu.roll(t, LANES - HEAD_DIM // 2, 1),
                        pltpu.roll(t, HEAD_DIM // 2, 1))
        return t * cos + rot * sin

    def duplicate_heads(t, o_ref, off):
        swapped = pltpu.roll(t, HEAD_DIM, 1)
        o_ref[:, off:off + LANES] = jnp.where(low_head, t, swapped).astype(BF16)
        o_ref[:, off + LANES:off + 2 * LANES] = jnp.where(low_head, swapped, t).astype(BF16)

    for c in range(D_ATT // LANES):
        t = z[:, OFF_Q + c * LANES:OFF_Q + (c + 1) * LANES]
        q_ref[:, c * LANES:(c + 1) * LANES] = (rope(t) * (HEAD_DIM ** -0.5 * LOG2_E)).astype(BF16)
    duplicate_heads(rope(z[:, OFF_K:OFF_K + LANES]), kv_ref, 0)
    duplicate_heads(z[:, OFF_V:OFF_V + LANES], kv_ref, N_KV_HEADS * LANES)


def _inproj(x, w, cos, sin, seq, layer):
    rows = x.shape[0]
    tiles_per_seq = seq // INPROJ_ROWS
    row_spec = lambda width: pl.BlockSpec((INPROJ_ROWS, width), lambda i: (i, 0))
    pos_spec = pl.BlockSpec((INPROJ_ROWS, LANES), lambda i: (i % tiles_per_seq, 0))
    return pl.pallas_call(
        _inproj_body,
        grid=(rows // INPROJ_ROWS,),
        in_specs=[row_spec(D_MODEL), _layer_weight_spec(layer, (D_MODEL, D_IN)), pos_spec, pos_spec],
        out_specs=[row_spec(D_SC), row_spec(D_SC + D_CC), row_spec(D_ATT), row_spec(KV_WIDTH)],
        out_shape=[
            jax.ShapeDtypeStruct((rows, D_SC), F32),
            jax.ShapeDtypeStruct((rows, D_SC + D_CC), F32),
            jax.ShapeDtypeStruct((rows, D_ATT), BF16),
            jax.ShapeDtypeStruct((rows, KV_WIDTH), BF16),
        ],
        compiler_params=pltpu.CompilerParams(
            dimension_semantics=("arbitrary",), vmem_limit_bytes=VMEM_LIMIT_BYTES),
        name="inproj",
    )(x, w, cos, sin)


def _mixer_body(sink_ref, sb_ref, cvp_ref, cv_ref, cvn_ref, q_ref, kvp_ref, kv_ref, kvn_ref,
                scw_ref, ccw_ref, ccb_ref, ccg_ref, ccbeta_ref, wout_ref,
                o_ref, cvx_ref, shf_ref, kvx_ref, ycat_ref, *, tiles_per_seq):
    it = pl.program_id(0) % tiles_per_seq
    first = it == 0
    last = it == tiles_per_seq - 1
    n_blocks = MIX_ROWS // BLOCK
    ext_rows = MIX_ROWS + 2 * HALO_ROWS

    cvx_ref[0:HALO_ROWS, :] = jnp.where(first, 0.0, cvp_ref[...])
    cvx_ref[HALO_ROWS:HALO_ROWS + MIX_ROWS, :] = cv_ref[...]
    cvx_ref[HALO_ROWS + MIX_ROWS:, :] = jnp.where(last, 0.0, cvn_ref[...])
    ext_u = cvx_ref[:, D_SC:]
    for r in range(1, SUBLANES):
        shf_ref[r - 1, :, D_SC:] = pltpu.roll(ext_u, ext_rows - r, 0)
    ext_p = cvx_ref[:, :D_SC]
    for r in sorted({(HALO_ROWS + k - 1) % SUBLANES for k in range(SC_WIDTH)} - {0}):
        shf_ref[r - 1, :, :D_SC] = pltpu.roll(ext_p, ext_rows - r, 0)

    def shifted(row, lo_lane, hi_lane):
        r = row % SUBLANES
        if r == 0:
            return cvx_ref[row:row + CONV_ROWS, lo_lane:hi_lane]
        return shf_ref[r - 1, row - r:row - r + CONV_ROWS, lo_lane:hi_lane]

    lane = lax.broadcasted_iota(jnp.int32, (BLOCK, LANES), 1)
    low = lane < HEAD_DIM
    for src_ref, row0, n in ((kvp_ref, 0, BLOCK), (kv_ref, BLOCK, MIX_ROWS),
                             (kvn_ref, BLOCK + MIX_ROWS, BLOCK)):
        for rb in range(n // BLOCK):
            for c in range(KV_WIDTH // LANES):
                blk = src_ref[rb * BLOCK:(rb + 1) * BLOCK, c * LANES:(c + 1) * LANES]
                rows = slice(row0 + rb * BLOCK, row0 + (rb + 1) * BLOCK)
                kvx_ref[rows, 2 * c * LANES:(2 * c + 1) * LANES] = jnp.where(low, blk, 0)
                kvx_ref[rows, (2 * c + 1) * LANES:(2 * c + 2) * LANES] = jnp.where(low, 0, blk)

    ccb = ccb_ref[...]
    ccg = ccg_ref[...]
    ccbeta = ccbeta_ref[...]

    def tap_weight(w_ref, k):
        return jnp.tile(w_ref[k], (CONV_ROWS // SUBLANES, 1))

    def conv_chunk(rc):
        base = HALO_ROWS + rc * CONV_ROWS
        acc = None
        for k in range(SC_WIDTH):
            tap = shifted(base + k - 1, 0, D_SC) * tap_weight(scw_ref, k)
            acc = tap if acc is None else acc + tap
        y_sc = sb_ref[rc * CONV_ROWS:(rc + 1) * CONV_ROWS, :] * acc
        ycat_ref[rc * CONV_ROWS:(rc + 1) * CONV_ROWS, 0:D_SC] = y_sc.astype(BF16)
        acc = None
        for k in range(CC_WIDTH):
            tap = shifted(base + k - CC_PAD, D_SC, D_SC + D_CC) * tap_weight(ccw_ref, k)
            acc = tap if acc is None else acc + tap
        u = _layer_norm(acc + ccb, ccg, ccbeta)
        y_cc = u * jax.nn.sigmoid(u)
        ycat_ref[rc * CONV_ROWS:(rc + 1) * CONV_ROWS, D_SC + D_ATT:] = y_cc.astype(BF16)
        return _fold_to_vreg(y_sc) + _fold_to_vreg(y_cc)

    qi = lax.broadcasted_iota(jnp.int32, (BLOCK, BLOCK), 0)
    kj = lax.broadcasted_iota(jnp.int32, (BLOCK, BLOCK), 1)
    band_prev = kj >= qi
    band_next = kj <= qi
    low_pair = lax.broadcasted_iota(jnp.int32, (BLOCK, LANES), 1) < HEAD_DIM
    contract_lanes = (((1,), (1,)), ((), ()))

    def attention(j, kvh, anchor):
        valid_prev = band_prev & jnp.logical_not(first) if j == 0 else band_prev
        valid_next = band_next & jnp.logical_not(last) if j == n_blocks - 1 else band_next
        if anchor is None:
            masked = jnp.full((BLOCK, BLOCK), MASK_VALUE, F32)
        else:
            masked = jnp.tile(MASK_VALUE + _zeros_from(anchor), (BLOCK // SUBLANES, 1))
        q_rows = slice(j * BLOCK, (j + 1) * BLOCK)
        k_rows = slice(j * BLOCK, (j + 3) * BLOCK)
        qp = jnp.concatenate(
            [q_ref[q_rows, (2 * kvh) * LANES:(2 * kvh + 1) * LANES],
             q_ref[q_rows, (2 * kvh + 1) * LANES:(2 * kvh + 2) * LANES]], axis=0)
        probs = []
        rinv = []
        for parity in range(2):
            k_blk = kvx_ref[k_rows, (2 * kvh + parity) * LANES:(2 * kvh + parity + 1) * LANES]
            s_all = lax.dot_general(qp, k_blk, contract_lanes, preferred_element_type=F32)
            p_slabs = []
            for pair in range(2):
                h = GQA_GROUP * kvh + 2 * pair + parity
                rows = slice(pair * BLOCK, (pair + 1) * BLOCK)
                s_prev = jnp.where(valid_prev, s_all[rows, 0:BLOCK], masked)
                s_own = s_all[rows, BLOCK:2 * BLOCK]
                s_next = jnp.where(valid_next, s_all[rows, 2 * BLOCK:3 * BLOCK], masked)
                sink = sink_ref[h] * LOG2_E
                row_max = jnp.max(jnp.maximum(jnp.maximum(s_prev, s_own), s_next), axis=-1, keepdims=True)
                m = jnp.maximum(row_max, sink)
                p_prev = jnp.exp2(s_prev - m)
                p_own = jnp.exp2(s_own - m)
                p_next = jnp.exp2(s_next - m)
                denom = jnp.sum(p_prev + p_own + p_next, axis=-1, keepdims=True) + jnp.exp2(sink - m)
                rinv.append(1.0 / denom)
                p_slabs.append(jnp.concatenate(
                    [p_prev.astype(BF16), p_own.astype(BF16), p_next.astype(BF16)], axis=1))
            probs.append(jnp.concatenate(p_slabs, axis=0))
        v_off = 2 * N_KV_HEADS * LANES + 2 * kvh * LANES
        o = (jnp.dot(probs[0], kvx_ref[k_rows, v_off:v_off + LANES], preferred_element_type=F32)
             + jnp.dot(probs[1], kvx_ref[k_rows, v_off + LANES:v_off + 2 * LANES],
                       preferred_element_type=F32))
        for pair in range(2):
            scale = jnp.where(low_pair, rinv[pair], rinv[2 + pair])
            col = D_SC + (2 * kvh + pair) * LANES
            ycat_ref[q_rows, col:col + LANES] = (
                o[pair * BLOCK:(pair + 1) * BLOCK, :] * scale).astype(BF16)
        return _fold_to_vreg(o)

    n_steps = n_blocks * N_KV_HEADS
    assert MIX_ROWS // CONV_ROWS == n_steps
    anchors = []
    for step in range(n_steps):
        anchor = anchors[step - ATT_LAG] if step >= ATT_LAG else None
        done = attention(step // N_KV_HEADS, step % N_KV_HEADS, anchor)
        anchors.append(done + conv_chunk(step))

    o_ref[...] = jnp.dot(ycat_ref[...], wout_ref[...], preferred_element_type=F32)


def _mixer(sb, cv, q, kv, sink, scw, ccw, ccb, ccg, ccbeta, wout, seq, layer):
    rows = sb.shape[0]
    tiles_per_seq = seq // MIX_ROWS
    n_tiles = rows // MIX_ROWS
    halo_per_tile = MIX_ROWS // HALO_ROWS
    blocks_per_tile = MIX_ROWS // BLOCK
    row_spec = lambda width: pl.BlockSpec((MIX_ROWS, width), lambda i: (i, 0))
    cv_prev = pl.BlockSpec((HALO_ROWS, D_SC + D_CC),
                           lambda i: (jnp.maximum(i * halo_per_tile - 1, 0), 0))
    cv_next = pl.BlockSpec((HALO_ROWS, D_SC + D_CC),
                           lambda i: (jnp.minimum((i + 1) * halo_per_tile, n_tiles * halo_per_tile - 1), 0))
    kv_prev = pl.BlockSpec((BLOCK, KV_WIDTH),
                           lambda i: (jnp.maximum(i * blocks_per_tile - 1, 0), 0))
    kv_next = pl.BlockSpec((BLOCK, KV_WIDTH),
                           lambda i: (jnp.minimum((i + 1) * blocks_per_tile, n_tiles * blocks_per_tile - 1), 0))
    ext_rows = MIX_ROWS + 2 * HALO_ROWS
    return pl.pallas_call(
        functools.partial(_mixer_body, tiles_per_seq=tiles_per_seq),
        grid=(n_tiles,),
        in_specs=[
            pl.BlockSpec(memory_space=pltpu.SMEM),
            row_spec(D_SC),
            cv_prev, row_spec(D_SC + D_CC), cv_next,
            row_spec(D_ATT),
            kv_prev, row_spec(KV_WIDTH), kv_next,
            _const_spec((SC_WIDTH, SUBLANES, D_SC)), _const_spec((CC_WIDTH, SUBLANES, D_CC)),
            _const_spec((1, D_CC)), _const_spec((1, D_CC)), _const_spec((1, D_CC)),
            _layer_weight_spec(layer, (D_MODEL, D_MODEL)),
        ],
        out_specs=row_spec(D_MODEL),
        out_shape=jax.ShapeDtypeStruct((rows, D_MODEL), F32),
        scratch_shapes=[
            pltpu.VMEM((ext_rows, D_SC + D_CC), F32),
            pltpu.VMEM((SUBLANES - 1, ext_rows, D_SC + D_CC), F32),
            pltpu.VMEM((MIX_ROWS + 2 * BLOCK, 2 * KV_WIDTH), BF16),
            pltpu.VMEM((MIX_ROWS, D_MODEL), BF16),
        ],
        compiler_params=pltpu.CompilerParams(
            dimension_semantics=("arbitrary",), vmem_limit_bytes=VMEM_LIMIT_BYTES),
        name="mixer",
    )(sink, sb, cv, cv, cv, q, kv, kv, kv, scw, ccw, ccb, ccg, ccbeta, wout)


def _rope_tables(seq):
    half = HEAD_DIM // 2
    inv_freq = ROPE_THETA ** (-jnp.arange(half, dtype=F32) / half)
    ang = jnp.arange(seq).astype(F32)[:, None] * inv_freq[None, :]
    cos = jnp.cos(ang)
    sin = jnp.sin(ang)
    reps = LANES // HEAD_DIM
    cos_t = jnp.tile(jnp.concatenate([cos, cos], axis=-1), (1, reps))
    sin_t = jnp.tile(jnp.concatenate([-sin, sin], axis=-1), (1, reps))
    return cos_t, sin_t


def kernel(x, ffn1_w_gu, ffn1_w_down, ln1_g, ln1_b, w_in, sc_conv_w, attn_sink, cc_conv_w, cc_conv_b,
           cc_ln_g, cc_ln_b, w_out, ln2_g, ln2_b, ffn2_w_gu, ffn2_w_down, ln3_g, ln3_b):
    batch, seq, d = x.shape
    assert d == D_MODEL and seq % MIX_ROWS == 0 and seq % INPROJ_ROWS == 0
    cos_t, sin_t = _rope_tables(seq)
    row = lambda v: v.reshape(1, -1)
    h = x.reshape(batch * seq, d)
    wgu1, wd1, wgu2, wd2 = (w.astype(BF16) for w in (ffn1_w_gu, ffn1_w_down, ffn2_w_gu, ffn2_w_down))
    win, wout = w_in.astype(BF16), w_out.astype(BF16)
    taps = lambda w: jnp.broadcast_to(w[:, None, :], (w.shape[0], SUBLANES, w.shape[1]))
    for l in range(DEPTH):
        h = _ffn(h, wgu1, wd1, row(ln1_g[l]), row(ln1_b[l]), l)
        sb, cv, q, kv = _inproj(h, win, cos_t, sin_t, seq, l)
        mixed = _mixer(sb, cv, q, kv, attn_sink[l], taps(sc_conv_w[l]), taps(cc_conv_w[l]),
                       row(cc_conv_b[l]), row(cc_ln_g[l]), row(cc_ln_b[l]), wout, seq, l)
        h = _ffn_prenorm(h, mixed, row(ln2_g[l]), row(ln2_b[l]), wgu2, wd2,
                         row(ln3_g[l]), row(ln3_b[l]), l)
    return h.reshape(batch, seq, d)
```

```python
import functools

import jax
import jax.numpy as jnp
from jax import lax
from jax.experimental import pallas as pl
from jax.experimental.pallas import tpu as pltpu

D_MODEL = 1024
DEPTH = 2
HEAD_DIM = 64
D_SC = 256
D_ATT = 512
D_CC = 256
N_Q_HEADS = 8
N_KV_HEADS = 2
GQA_GROUP = N_Q_HEADS // N_KV_HEADS
SC_WIDTH = 3
CC_WIDTH = 31
CC_PAD = (CC_WIDTH - 1) // 2
WINDOW = 128
BLOCK = 128
ROPE_THETA = 10000.0
D_FF = 2816
LN_EPS = 1e-5
ALPHA = (2.0 * DEPTH) ** 0.25
D_IN = 3 * D_SC + D_ATT + 2 * N_KV_HEADS * HEAD_DIM + 2 * D_CC
MASK_VALUE = -1e30

OFF_SC_B = 0
OFF_SC_C = D_SC
OFF_SC_H = 2 * D_SC
OFF_Q = 3 * D_SC
OFF_K = OFF_Q + D_ATT
OFF_V = OFF_K + N_KV_HEADS * HEAD_DIM
OFF_CC_A = OFF_V + N_KV_HEADS * HEAD_DIM
OFF_CC_G = OFF_CC_A + D_CC

LANES = 128
SUBLANES = 8
HALO_ROWS = 16
VMEM_LIMIT_BYTES = 56 * 1024 * 1024

FFN_ROWS = 512
INPROJ_ROWS = 1024
FFN_CHUNK = 256
N_CHUNKS = D_FF // FFN_CHUNK
OUT_NORM_CHUNKS = (1, 2, 3, 4)
PRE_NORM_CHUNKS = (5, 6, 7, 8)
MIX_ROWS = 1024
CONV_ROWS = 8
KV_WIDTH = 2 * N_KV_HEADS * LANES
ATT_LAG = 2
LOG2_E = 1.4426950408889634

F32 = jnp.float32
BF16 = jnp.bfloat16


def _layer_norm(y, g, b):
    mu = jnp.mean(y, axis=-1, keepdims=True)
    yc = y - mu
    var = jnp.mean(yc * yc, axis=-1, keepdims=True)
    return yc * lax.rsqrt(var + LN_EPS) * g + b


def _const_spec(shape):
    return pl.BlockSpec(shape, lambda i: (0,) * len(shape))


def _layer_weight_spec(layer, shape):
    return pl.BlockSpec((None,) + shape, lambda i: (layer,) + (0,) * len(shape),
                        pipeline_mode=pl.Buffered(1))


def _zeros_from(anchor):
    bits = lax.bitcast_convert_type(anchor, jnp.uint32)
    zero = lax.shift_right_logical(lax.shift_right_logical(bits, jnp.uint32(16)), jnp.uint32(16))
    return lax.bitcast_convert_type(zero, F32)


def _fold_to_vreg(v):
    acc = v[:, 0:LANES]
    for c in range(1, v.shape[1] // LANES):
        acc = acc + v[:, c * LANES:(c + 1) * LANES]
    out = acc[0:SUBLANES]
    for r in range(1, v.shape[0] // SUBLANES):
        out = out + acc[r * SUBLANES:(r + 1) * SUBLANES]
    return out


def _lane_block_sum(v):
    acc = v[:, 0:LANES]
    for c in range(1, v.shape[1] // LANES):
        acc = acc + v[:, c * LANES:(c + 1) * LANES]
    return acc


def _swiglu_residual(x, wgu_ref, wd_ref, anchors=None):
    xb = x.astype(BF16)
    acc = None
    for c in range(N_CHUNKS):
        g = jnp.dot(xb, wgu_ref[:, c * FFN_CHUNK:(c + 1) * FFN_CHUNK], preferred_element_type=F32)
        u = jnp.dot(xb, wgu_ref[:, D_FF + c * FFN_CHUNK:D_FF + (c + 1) * FFN_CHUNK],
                    preferred_element_type=F32)
        h = g * jax.nn.sigmoid(g) * u
        if anchors and c in anchors:
            zeros = _zeros_from(anchors[c])
            h = h + jnp.tile(zeros, (h.shape[0] // zeros.shape[0], FFN_CHUNK // LANES))
        d = jnp.dot(h.astype(BF16), wd_ref[c * FFN_CHUNK:(c + 1) * FFN_CHUNK, :],
                    preferred_element_type=F32)
        acc = d if acc is None else acc + d
    return ALPHA * x + 0.5 * acc


def _in_row_parts(compute, dst_ref, chunks):
    part = FFN_ROWS // len(chunks)
    anchors = {}
    for p, c in enumerate(chunks):
        rows = slice(p * part, (p + 1) * part)
        value = compute(rows)
        dst_ref[rows, :] = value
        anchors[c] = _lane_block_sum(value)
    return anchors


def _ffn_body(x_ref, wgu_ref, wd_ref, g_ref, b_ref, o_ref, y_ref):
    i = pl.program_id(0)
    n_tiles = pl.num_programs(0) - 1
    out_norm = lambda rows: _layer_norm(y_ref[rows, :], g_ref[...], b_ref[...])

    @pl.when(i == 0)
    def _():
        y_ref[...] = _swiglu_residual(x_ref[...], wgu_ref, wd_ref)

    @pl.when((i > 0) & (i < n_tiles))
    def _():
        anchors = _in_row_parts(out_norm, o_ref, OUT_NORM_CHUNKS)
        y_ref[...] = _swiglu_residual(x_ref[...], wgu_ref, wd_ref, anchors)

    @pl.when(i == n_tiles)
    def _():
        o_ref[...] = out_norm(slice(None))


def _ffn_prenorm_body(x_ref, m_ref, pg_ref, pb_ref, wgu_ref, wd_ref, g_ref, b_ref, o_ref, xn_ref, y_ref):
    j = pl.program_id(0)
    n_tiles = pl.num_programs(0) - 2
    out_norm = lambda rows: _layer_norm(y_ref[rows, :], g_ref[...], b_ref[...])
    pre_norm = lambda rows: _layer_norm(ALPHA * x_ref[rows, :] + m_ref[rows, :], pg_ref[...], pb_ref[...])

    @pl.when(j == 0)
    def _():
        xn_ref[0] = pre_norm(slice(None))
        y_ref[...] = jnp.zeros((FFN_ROWS, D_MODEL), F32)

    @pl.when((j > 0) & (j <= n_tiles))
    def _():
        x_cur = xn_ref[(j - 1) % 2]
        anchors = _in_row_parts(out_norm, o_ref, OUT_NORM_CHUNKS)
        anchors.update(_in_row_parts(pre_norm, xn_ref.at[j % 2], PRE_NORM_CHUNKS))
        y_ref[...] = _swiglu_residual(x_cur, wgu_ref, wd_ref, anchors)

    @pl.when(j == n_tiles + 1)
    def _():
        o_ref[...] = out_norm(slice(None))


def _ffn(x, wgu, wd, g, b, layer):
    rows = x.shape[0]
    n_tiles = rows // FFN_ROWS
    return pl.pallas_call(
        _ffn_body,
        grid=(n_tiles + 1,),
        in_specs=[
            pl.BlockSpec((FFN_ROWS, D_MODEL), lambda i: (jnp.minimum(i, n_tiles - 1), 0)),
            _layer_weight_spec(layer, (D_MODEL, 2 * D_FF)),
            _layer_weight_spec(layer, (D_FF, D_MODEL)),
            _const_spec((1, D_MODEL)),
            _const_spec((1, D_MODEL)),
        ],
        out_specs=pl.BlockSpec((FFN_ROWS, D_MODEL), lambda i: (jnp.maximum(i - 1, 0), 0)),
        out_shape=jax.ShapeDtypeStruct((rows, D_MODEL), F32),
        scratch_shapes=[pltpu.VMEM((FFN_ROWS, D_MODEL), F32)],
        compiler_params=pltpu.CompilerParams(
            dimension_semantics=("arbitrary",), vmem_limit_bytes=VMEM_LIMIT_BYTES),
        name="ffn",
    )(x, wgu, wd, g, b)


def _ffn_prenorm(x, mixed, pre_g, pre_b, wgu, wd, g, b, layer):
    rows = x.shape[0]
    n_tiles = rows // FFN_ROWS
    in_tile = pl.BlockSpec((FFN_ROWS, D_MODEL), lambda j: (jnp.minimum(j, n_tiles - 1), 0))
    return pl.pallas_call(
        _ffn_prenorm_body,
        grid=(n_tiles + 2,),
        in_specs=[
            in_tile, in_tile,
            _const_spec((1, D_MODEL)), _const_spec((1, D_MODEL)),
            _layer_weight_spec(layer, (D_MODEL, 2 * D_FF)),
            _layer_weight_spec(layer, (D_FF, D_MODEL)),
            _const_spec((1, D_MODEL)), _const_spec((1, D_MODEL)),
        ],
        out_specs=pl.BlockSpec((FFN_ROWS, D_MODEL), lambda j: (jnp.clip(j - 2, 0, n_tiles - 1), 0)),
        out_shape=jax.ShapeDtypeStruct((rows, D_MODEL), F32),
        scratch_shapes=[pltpu.VMEM((2, FFN_ROWS, D_MODEL), F32), pltpu.VMEM((FFN_ROWS, D_MODEL), F32)],
        compiler_params=pltpu.CompilerParams(
            dimension_semantics=("arbitrary",), vmem_limit_bytes=VMEM_LIMIT_BYTES),
        name="ffn_prenorm",
    )(x, mixed, pre_g, pre_b, wgu, wd, g, b)


def _inproj_body(x_ref, w_ref, cos_ref, sin_ref, sb_ref, cv_ref, q_ref, kv_ref):
    xb = x_ref[...].astype(BF16)
    z = jnp.dot(xb, w_ref[...], preferred_element_type=F32)
    sb_ref[...] = z[:, OFF_SC_B:OFF_SC_B + D_SC]
    cv_ref[:, 0:D_SC] = z[:, OFF_SC_C:OFF_SC_C + D_SC] * z[:, OFF_SC_H:OFF_SC_H + D_SC]
    cv_ref[:, D_SC:D_SC + D_CC] = (z[:, OFF_CC_A:OFF_CC_A + D_CC]
                                   * jax.nn.sigmoid(z[:, OFF_CC_G:OFF_CC_G + D_CC]))
    cos = cos_ref[...]
    sin = sin_ref[...]
    lane = lax.broadcasted_iota(jnp.int32, cos.shape, 1)
    first_half = (lane % HEAD_DIM) < (HEAD_DIM // 2)
    low_head = lane < HEAD_DIM

    def rope(t):
        rot = jnp.where(first_half, pltpu.roll(t, LANES - HEAD_DIM // 2, 1),
                        pltpu.roll(t, HEAD_DIM // 2, 1))
        return t * cos + rot * sin

    def duplicate_heads(t, o_ref, off):
        swapped = pltpu.roll(t, HEAD_DIM, 1)
        o_ref[:, off:off + LANES] = jnp.where(low_head, t, swapped).astype(BF16)
        o_ref[:, off + LANES:off + 2 * LANES] = jnp.where(low_head, swapped, t).astype(BF16)

    for c in range(D_ATT // LANES):
        t = z[:, OFF_Q + c * LANES:OFF_Q + (c + 1) * LANES]
        q_ref[:, c * LANES:(c + 1) * LANES] = (rope(t) * (HEAD_DIM ** -0.5 * LOG2_E)).astype(BF16)
    duplicate_heads(rope(z[:, OFF_K:OFF_K + LANES]), kv_ref, 0)
    duplicate_heads(z[:, OFF_V:OFF_V + LANES], kv_ref, N_KV_HEADS * LANES)


def _inproj(x, w, cos, sin, seq, layer):
    rows = x.shape[0]
    tiles_per_seq = seq // INPROJ_ROWS
    row_spec = lambda width: pl.BlockSpec((INPROJ_ROWS, width), lambda i: (i, 0))
    pos_spec = pl.BlockSpec((INPROJ_ROWS, LANES), lambda i: (i % tiles_per_seq, 0))
    return pl.pallas_call(
        _inproj_body,
        grid=(rows // INPROJ_ROWS,),
        in_specs=[row_spec(D_MODEL), _layer_weight_spec(layer, (D_MODEL, D_IN)), pos_spec, pos_spec],
        out_specs=[row_spec(D_SC), row_spec(D_SC + D_CC), row_spec(D_ATT), row_spec(KV_WIDTH)],
        out_shape=[
            jax.ShapeDtypeStruct((rows, D_SC), F32),
            jax.ShapeDtypeStruct((rows, D_SC + D_CC), F32),
            jax.ShapeDtypeStruct((rows, D_ATT), BF16),
            jax.ShapeDtypeStruct((rows, KV_WIDTH), BF16),
        ],
        compiler_params=pltpu.CompilerParams(
            dimension_semantics=("arbitrary",), vmem_limit_bytes=VMEM_LIMIT_BYTES),
        name="inproj",
    )(x, w, cos, sin)


def _mixer_body(sink_ref, sb_ref, cvp_ref, cv_ref, cvn_ref, q_ref, kvp_ref, kv_ref, kvn_ref,
                scw_ref, ccw_ref, ccb_ref, ccg_ref, ccbeta_ref, wout_ref,
                o_ref, cvx_ref, shf_ref, kvx_ref, ycat_ref, *, tiles_per_seq):
    it = pl.program_id(0) % tiles_per_seq
    first = it == 0
    last = it == tiles_per_seq - 1
    n_blocks = MIX_ROWS // BLOCK
    ext_rows = MIX_ROWS + 2 * HALO_ROWS

    cvx_ref[0:HALO_ROWS, :] = jnp.where(first, 0.0, cvp_ref[...])
    cvx_ref[HALO_ROWS:HALO_ROWS + MIX_ROWS, :] = cv_ref[...]
    cvx_ref[HALO_ROWS + MIX_ROWS:, :] = jnp.where(last, 0.0, cvn_ref[...])
    ext_u = cvx_ref[:, D_SC:]
    for r in range(1, SUBLANES):
        shf_ref[r - 1, :, D_SC:] = pltpu.roll(ext_u, ext_rows - r, 0)
    ext_p = cvx_ref[:, :D_SC]
    for r in sorted({(HALO_ROWS + k - 1) % SUBLANES for k in range(SC_WIDTH)} - {0}):
        shf_ref[r - 1, :, :D_SC] = pltpu.roll(ext_p, ext_rows - r, 0)

    def shifted(row, lo_lane, hi_lane):
        r = row % SUBLANES
        if r == 0:
            return cvx_ref[row:row + CONV_ROWS, lo_lane:hi_lane]
        return shf_ref[r - 1, row - r:row - r + CONV_ROWS, lo_lane:hi_lane]

    lane = lax.broadcasted_iota(jnp.int32, (BLOCK, LANES), 1)
    low = lane < HEAD_DIM
    for src_ref, row0, n in ((kvp_ref, 0, BLOCK), (kv_ref, BLOCK, MIX_ROWS),
                             (kvn_ref, BLOCK + MIX_ROWS, BLOCK)):
        for rb in range(n // BLOCK):
            for c in range(KV_WIDTH // LANES):
                blk = src_ref[rb * BLOCK:(rb + 1) * BLOCK, c * LANES:(c + 1) * LANES]
                rows = slice(row0 + rb * BLOCK, row0 + (rb + 1) * BLOCK)
                kvx_ref[rows, 2 * c * LANES:(2 * c + 1) * LANES] = jnp.where(low, blk, 0)
                kvx_ref[rows, (2 * c + 1) * LANES:(2 * c + 2) * LANES] = jnp.where(low, 0, blk)

    ccb = ccb_ref[...]
    ccg = ccg_ref[...]
    ccbeta = ccbeta_ref[...]

    def tap_weight(w_ref, k):
        return jnp.tile(w_ref[k], (CONV_ROWS // SUBLANES, 1))

    def conv_chunk(rc, after):
        base = HALO_ROWS + rc * CONV_ROWS
        acc = None
        for k in range(SC_WIDTH):
            tap = shifted(base + k - 1, 0, D_SC) * tap_weight(scw_ref, k)
            acc = tap if acc is None else acc + tap
        y_sc = sb_ref[rc * CONV_ROWS:(rc + 1) * CONV_ROWS, :] * acc
        ycat_ref[rc * CONV_ROWS:(rc + 1) * CONV_ROWS, 0:D_SC] = y_sc.astype(BF16)
        acc = None
        for k in range(CC_WIDTH):
            tap = shifted(base + k - CC_PAD, D_SC, D_SC + D_CC) * tap_weight(ccw_ref, k)
            if k == 0 and after is not None:
                tap = tap + _zeros_from(after)
            acc = tap if acc is None else acc + tap
        u = _layer_norm(acc + ccb, ccg, ccbeta)
        y_cc = u * jax.nn.sigmoid(u)
        ycat_ref[rc * CONV_ROWS:(rc + 1) * CONV_ROWS, D_SC + D_ATT:] = y_cc.astype(BF16)
        return acc, _fold_to_vreg(y_sc) + _fold_to_vreg(y_cc)

    qi = lax.broadcasted_iota(jnp.int32, (BLOCK, BLOCK), 0)
    kj = lax.broadcasted_iota(jnp.int32, (BLOCK, BLOCK), 1)
    band_prev = kj >= qi
    band_next = kj <= qi
    low_pair = lax.broadcasted_iota(jnp.int32, (BLOCK, LANES), 1) < HEAD_DIM
    contract_lanes = (((1,), (1,)), ((), ()))

    def attention(j, kvh, anchor):
        valid_prev = band_prev & jnp.logical_not(first) if j == 0 else band_prev
        valid_next = band_next & jnp.logical_not(last) if j == n_blocks - 1 else band_next
        if anchor is None:
            masked = jnp.full((BLOCK, BLOCK), MASK_VALUE, F32)
        else:
            masked = jnp.tile(MASK_VALUE + _zeros_from(anchor), (BLOCK // SUBLANES, 1))
        q_rows = slice(j * BLOCK, (j + 1) * BLOCK)
        k_rows = slice(j * BLOCK, (j + 3) * BLOCK)
        qp = jnp.concatenate(
            [q_ref[q_rows, (2 * kvh) * LANES:(2 * kvh + 1) * LANES],
             q_ref[q_rows, (2 * kvh + 1) * LANES:(2 * kvh + 2) * LANES]], axis=0)
        probs = []
        rinv = []
        for parity in range(2):
            k_blk = kvx_ref[k_rows, (2 * kvh + parity) * LANES:(2 * kvh + parity + 1) * LANES]
            s_all = lax.dot_general(qp, k_blk, contract_lanes, preferred_element_type=F32)
            p_slabs = []
            for pair in range(2):
                h = GQA_GROUP * kvh + 2 * pair + parity
                rows = slice(pair * BLOCK, (pair + 1) * BLOCK)
                s_prev = jnp.where(valid_prev, s_all[rows, 0:BLOCK], masked)
                s_own = s_all[rows, BLOCK:2 * BLOCK]
                s_next = jnp.where(valid_next, s_all[rows, 2 * BLOCK:3 * BLOCK], masked)
                sink = sink_ref[h] * LOG2_E
                row_max = jnp.max(jnp.maximum(jnp.maximum(s_prev, s_own), s_next), axis=-1, keepdims=True)
                m = jnp.maximum(row_max, sink)
                p_prev = jnp.exp2(s_prev - m)
                p_own = jnp.exp2(s_own - m)
                p_next = jnp.exp2(s_next - m)
                denom = jnp.sum(p_prev + p_own + p_next, axis=-1, keepdims=True) + jnp.exp2(sink - m)
                rinv.append(1.0 / denom)
                p_slabs.append(jnp.concatenate(
                    [p_prev.astype(BF16), p_own.astype(BF16), p_next.astype(BF16)], axis=1))
            probs.append(jnp.concatenate(p_slabs, axis=0))
        v_off = 2 * N_KV_HEADS * LANES + 2 * kvh * LANES
        o = (jnp.dot(probs[0], kvx_ref[k_rows, v_off:v_off + LANES], preferred_element_type=F32)
             + jnp.dot(probs[1], kvx_ref[k_rows, v_off + LANES:v_off + 2 * LANES],
                       preferred_element_type=F32))
        for pair in range(2):
            scale = jnp.where(low_pair, rinv[pair], rinv[2 + pair])
            col = D_SC + (2 * kvh + pair) * LANES
            ycat_ref[q_rows, col:col + LANES] = (
                o[pair * BLOCK:(pair + 1) * BLOCK, :] * scale).astype(BF16)
        return _fold_to_vreg(o)

    n_steps = n_blocks * N_KV_HEADS
    chunks_per_step = MIX_ROWS // CONV_ROWS // n_steps
    anchors = []
    acc = None
    for step in range(n_steps):
        anchor = anchors[step - ATT_LAG] if step >= ATT_LAG else None
        done = attention(step // N_KV_HEADS, step % N_KV_HEADS, anchor)
        for c in range(chunks_per_step):
            acc, stored = conv_chunk(step * chunks_per_step + c, acc)
            done = done + stored
        anchors.append(done)

    o_ref[...] = jnp.dot(ycat_ref[...], wout_ref[...], preferred_element_type=F32)


def _mixer(sb, cv, q, kv, sink, scw, ccw, ccb, ccg, ccbeta, wout, seq, layer):
    rows = sb.shape[0]
    tiles_per_seq = seq // MIX_ROWS
    n_tiles = rows // MIX_ROWS
    halo_per_tile = MIX_ROWS // HALO_ROWS
    blocks_per_tile = MIX_ROWS // BLOCK
    row_spec = lambda width: pl.BlockSpec((MIX_ROWS, width), lambda i: (i, 0))
    cv_prev = pl.BlockSpec((HALO_ROWS, D_SC + D_CC),
                           lambda i: (jnp.maximum(i * halo_per_tile - 1, 0), 0))
    cv_next = pl.BlockSpec((HALO_ROWS, D_SC + D_CC),
                           lambda i: (jnp.minimum((i + 1) * halo_per_tile, n_tiles * halo_per_tile - 1), 0))
    kv_prev = pl.BlockSpec((BLOCK, KV_WIDTH),
                           lambda i: (jnp.maximum(i * blocks_per_tile - 1, 0), 0))
    kv_next = pl.BlockSpec((BLOCK, KV_WIDTH),
                           lambda i: (jnp.minimum((i + 1) * blocks_per_tile, n_tiles * blocks_per_tile - 1), 0))
    ext_rows = MIX_ROWS + 2 * HALO_ROWS
    return pl.pallas_call(
        functools.partial(_mixer_body, tiles_per_seq=tiles_per_seq),
        grid=(n_tiles,),
        in_specs=[
            pl.BlockSpec(memory_space=pltpu.SMEM),
            row_spec(D_SC),
            cv_prev, row_spec(D_SC + D_CC), cv_next,
            row_spec(D_ATT),
            kv_prev, row_spec(KV_WIDTH), kv_next,
            _const_spec((SC_WIDTH, SUBLANES, D_SC)), _const_spec((CC_WIDTH, SUBLANES, D_CC)),
            _const_spec((1, D_CC)), _const_spec((1, D_CC)), _const_spec((1, D_CC)),
            _layer_weight_spec(layer, (D_MODEL, D_MODEL)),
        ],
        out_specs=row_spec(D_MODEL),
        out_shape=jax.ShapeDtypeStruct((rows, D_MODEL), F32),
        scratch_shapes=[
            pltpu.VMEM((ext_rows, D_SC + D_CC), F32),
            pltpu.VMEM((SUBLANES - 1, ext_rows, D_SC + D_CC), F32),
            pltpu.VMEM((MIX_ROWS + 2 * BLOCK, 2 * KV_WIDTH), BF16),
            pltpu.VMEM((MIX_ROWS, D_MODEL), BF16),
        ],
        compiler_params=pltpu.CompilerParams(
            dimension_semantics=("arbitrary",), vmem_limit_bytes=VMEM_LIMIT_BYTES),
        name="mixer",
    )(sink, sb, cv, cv, cv, q, kv, kv, kv, scw, ccw, ccb, ccg, ccbeta, wout)


def _rope_tables(seq):
    half = HEAD_DIM // 2
    inv_freq = ROPE_THETA ** (-jnp.arange(half, dtype=F32) / half)
    ang = jnp.arange(seq).astype(F32)[:, None] * inv_freq[None, :]
    cos = jnp.cos(ang)
    sin = jnp.sin(ang)
    reps = LANES // HEAD_DIM
    cos_t = jnp.tile(jnp.concatenate([cos, cos], axis=-1), (1, reps))
    sin_t = jnp.tile(jnp.concatenate([-sin, sin], axis=-1), (1, reps))
    return cos_t, sin_t


def kernel(x, ffn1_w_gu, ffn1_w_down, ln1_g, ln1_b, w_in, sc_conv_w, attn_sink, cc_conv_w, cc_conv_b,
           cc_ln_g, cc_ln_b, w_out, ln2_g, ln2_b, ffn2_w_gu, ffn2_w_down, ln3_g, ln3_b):
    batch, seq, d = x.shape
    assert d == D_MODEL and seq % MIX_ROWS == 0 and seq % INPROJ_ROWS == 0
    cos_t, sin_t = _rope_tables(seq)
    row = lambda v: v.reshape(1, -1)
    h = x.reshape(batch * seq, d)
    wgu1, wd1, wgu2, wd2 = (w.astype(BF16) for w in (ffn1_w_gu, ffn1_w_down, ffn2_w_gu, ffn2_w_down))
    win, wout = w_in.astype(BF16), w_out.astype(BF16)
    taps = lambda w: jnp.broadcast_to(w[:, None, :], (w.shape[0], SUBLANES, w.shape[1]))
    for l in range(DEPTH):
        h = _ffn(h, wgu1, wd1, row(ln1_g[l]), row(ln1_b[l]), l)
        sb, cv, q, kv = _inproj(h, win, cos_t, sin_t, seq, l)
        mixed = _mixer(sb, cv, q, kv, attn_sink[l], taps(sc_conv_w[l]), taps(cc_conv_w[l]),
                       row(cc_conv_b[l]), row(cc_ln_g[l]), row(cc_ln_b[l]), wout, seq, l)
        h = _ffn_prenorm(h, mixed, row(ln2_g[l]), row(ln2_b[l]), wgu2, wd2,
                         row(ln3_g[l]), row(ln3_b[l]), l)
    return h.reshape(batch, seq, d)
```

```python
import functools

import jax
import jax.numpy as jnp
from jax import lax
from jax.experimental import pallas as pl
from jax.experimental.pallas import tpu as pltpu

D_MODEL = 1024
DEPTH = 2
HEAD_DIM = 64
D_SC = 256
D_ATT = 512
D_CC = 256
N_Q_HEADS = 8
N_KV_HEADS = 2
GQA_GROUP = N_Q_HEADS // N_KV_HEADS
SC_WIDTH = 3
CC_WIDTH = 31
CC_PAD = (CC_WIDTH - 1) // 2
WINDOW = 128
BLOCK = 128
ROPE_THETA = 10000.0
D_FF = 2816
LN_EPS = 1e-5
ALPHA = (2.0 * DEPTH) ** 0.25
D_IN = 3 * D_SC + D_ATT + 2 * N_KV_HEADS * HEAD_DIM + 2 * D_CC
MASK_VALUE = -1e30

OFF_SC_B = 0
OFF_SC_C = D_SC
OFF_SC_H = 2 * D_SC
OFF_Q = 3 * D_SC
OFF_K = OFF_Q + D_ATT
OFF_V = OFF_K + N_KV_HEADS * HEAD_DIM
OFF_CC_A = OFF_V + N_KV_HEADS * HEAD_DIM
OFF_CC_G = OFF_CC_A + D_CC

LANES = 128
SUBLANES = 8
HALO_ROWS = 16
VMEM_LIMIT_BYTES = 56 * 1024 * 1024

FFN_ROWS = 512
INPROJ_ROWS = 1024
FFN_CHUNK = 256
N_CHUNKS = D_FF // FFN_CHUNK
OUT_NORM_CHUNKS = (1, 2, 3, 4)
PRE_NORM_CHUNKS = (5, 6, 7, 8)
MIX_ROWS = 1024
CONV_ROWS = 8
KV_WIDTH = 2 * N_KV_HEADS * LANES
ATT_LAG = 2
LOG2_E = 1.4426950408889634

F32 = jnp.float32
BF16 = jnp.bfloat16

assert WINDOW == BLOCK
assert CC_PAD <= HALO_ROWS and MIX_ROWS % BLOCK == 0 and FFN_ROWS % len(OUT_NORM_CHUNKS) == 0


def _layer_norm(y, g, b):
    mu = jnp.mean(y, axis=-1, keepdims=True)
    yc = y - mu
    var = jnp.mean(yc * yc, axis=-1, keepdims=True)
    return yc * lax.rsqrt(var + LN_EPS) * g + b


def _const_spec(shape):
    return pl.BlockSpec(shape, lambda i: (0,) * len(shape))


def _layer_weight_spec(layer, shape):
    return pl.BlockSpec((None,) + shape, lambda i: (layer,) + (0,) * len(shape),
                        pipeline_mode=pl.Buffered(1))


def _zeros_from(anchor):
    bits = lax.bitcast_convert_type(anchor, jnp.uint32)
    zero = lax.shift_right_logical(lax.shift_right_logical(bits, jnp.uint32(16)), jnp.uint32(16))
    return lax.bitcast_convert_type(zero, F32)


def _fold_to_vreg(v):
    acc = v[:, 0:LANES]
    for c in range(1, v.shape[1] // LANES):
        acc = acc + v[:, c * LANES:(c + 1) * LANES]
    out = acc[0:SUBLANES]
    for r in range(1, v.shape[0] // SUBLANES):
        out = out + acc[r * SUBLANES:(r + 1) * SUBLANES]
    return out


def _lane_block_sum(v):
    acc = v[:, 0:LANES]
    for c in range(1, v.shape[1] // LANES):
        acc = acc + v[:, c * LANES:(c + 1) * LANES]
    return acc


def _swiglu_residual(x, wgu_ref, wd_ref, anchors=None):
    xb = x.astype(BF16)
    acc = None
    for c in range(N_CHUNKS):
        g = jnp.dot(xb, wgu_ref[:, c * FFN_CHUNK:(c + 1) * FFN_CHUNK], preferred_element_type=F32)
        u = jnp.dot(xb, wgu_ref[:, D_FF + c * FFN_CHUNK:D_FF + (c + 1) * FFN_CHUNK],
                    preferred_element_type=F32)
        h = g * jax.nn.sigmoid(g) * u
        if anchors and c in anchors:
            zeros = _zeros_from(anchors[c])
            h = h + jnp.tile(zeros, (h.shape[0] // zeros.shape[0], FFN_CHUNK // LANES))
        d = jnp.dot(h.astype(BF16), wd_ref[c * FFN_CHUNK:(c + 1) * FFN_CHUNK, :],
                    preferred_element_type=F32)
        acc = d if acc is None else acc + d
    return ALPHA * x + 0.5 * acc


def _in_row_parts(compute, dst_ref, chunks):
    part = FFN_ROWS // len(chunks)
    anchors = {}
    for p, c in enumerate(chunks):
        rows = slice(p * part, (p + 1) * part)
        value = compute(rows)
        dst_ref[rows, :] = value
        anchors[c] = _lane_block_sum(value)
    return anchors


def _ffn_body(x_ref, wgu_ref, wd_ref, g_ref, b_ref, o_ref, y_ref):
    i = pl.program_id(0)
    n_tiles = pl.num_programs(0) - 1
    out_norm = lambda rows: _layer_norm(y_ref[rows, :], g_ref[...], b_ref[...])

    @pl.when(i == 0)
    def _():
        y_ref[...] = _swiglu_residual(x_ref[...], wgu_ref, wd_ref)

    @pl.when((i > 0) & (i < n_tiles))
    def _():
        anchors = _in_row_parts(out_norm, o_ref, OUT_NORM_CHUNKS)
        y_ref[...] = _swiglu_residual(x_ref[...], wgu_ref, wd_ref, anchors)

    @pl.when(i == n_tiles)
    def _():
        o_ref[...] = out_norm(slice(None))


def _ffn_prenorm_body(x_ref, m_ref, pg_ref, pb_ref, wgu_ref, wd_ref, g_ref, b_ref, o_ref, xn_ref, y_ref):
    j = pl.program_id(0)
    n_tiles = pl.num_programs(0) - 2
    out_norm = lambda rows: _layer_norm(y_ref[rows, :], g_ref[...], b_ref[...])
    pre_norm = lambda rows: _layer_norm(ALPHA * x_ref[rows, :] + m_ref[rows, :], pg_ref[...], pb_ref[...])

    @pl.when(j == 0)
    def _():
        xn_ref[0] = pre_norm(slice(None))
        y_ref[...] = jnp.zeros((FFN_ROWS, D_MODEL), F32)

    @pl.when((j > 0) & (j <= n_tiles))
    def _():
        x_cur = xn_ref[(j - 1) % 2]
        anchors = _in_row_parts(out_norm, o_ref, OUT_NORM_CHUNKS)
        anchors.update(_in_row_parts(pre_norm, xn_ref.at[j % 2], PRE_NORM_CHUNKS))
        y_ref[...] = _swiglu_residual(x_cur, wgu_ref, wd_ref, anchors)

    @pl.when(j == n_tiles + 1)
    def _():
        o_ref[...] = out_norm(slice(None))


def _ffn(x, wgu, wd, g, b, layer):
    rows = x.shape[0]
    n_tiles = rows // FFN_ROWS
    return pl.pallas_call(
        _ffn_body,
        grid=(n_tiles + 1,),
        in_specs=[
            pl.BlockSpec((FFN_ROWS, D_MODEL), lambda i: (jnp.minimum(i, n_tiles - 1), 0)),
            _layer_weight_spec(layer, (D_MODEL, 2 * D_FF)),
            _layer_weight_spec(layer, (D_FF, D_MODEL)),
            _const_spec((1, D_MODEL)),
            _const_spec((1, D_MODEL)),
        ],
        out_specs=pl.BlockSpec((FFN_ROWS, D_MODEL), lambda i: (jnp.maximum(i - 1, 0), 0)),
        out_shape=jax.ShapeDtypeStruct((rows, D_MODEL), F32),
        scratch_shapes=[pltpu.VMEM((FFN_ROWS, D_MODEL), F32)],
        compiler_params=pltpu.CompilerParams(
            dimension_semantics=("arbitrary",), vmem_limit_bytes=VMEM_LIMIT_BYTES),
        name="ffn",
    )(x, wgu, wd, g, b)


def _ffn_prenorm(x, mixed, pre_g, pre_b, wgu, wd, g, b, layer):
    rows = x.shape[0]
    n_tiles = rows // FFN_ROWS
    in_tile = pl.BlockSpec((FFN_ROWS, D_MODEL), lambda j: (jnp.minimum(j, n_tiles - 1), 0))
    return pl.pallas_call(
        _ffn_prenorm_body,
        grid=(n_tiles + 2,),
        in_specs=[
            in_tile, in_tile,
            _const_spec((1, D_MODEL)), _const_spec((1, D_MODEL)),
            _layer_weight_spec(layer, (D_MODEL, 2 * D_FF)),
            _layer_weight_spec(layer, (D_FF, D_MODEL)),
            _const_spec((1, D_MODEL)), _const_spec((1, D_MODEL)),
        ],
        out_specs=pl.BlockSpec((FFN_ROWS, D_MODEL), lambda j: (jnp.clip(j - 2, 0, n_tiles - 1), 0)),
        out_shape=jax.ShapeDtypeStruct((rows, D_MODEL), F32),
        scratch_shapes=[pltpu.VMEM((2, FFN_ROWS, D_MODEL), F32), pltpu.VMEM((FFN_ROWS, D_MODEL), F32)],
        compiler_params=pltpu.CompilerParams(
            dimension_semantics=("arbitrary",), vmem_limit_bytes=VMEM_LIMIT_BYTES),
        name="ffn_prenorm",
    )(x, mixed, pre_g, pre_b, wgu, wd, g, b)


def _inproj_body(x_ref, w_ref, cos_ref, sin_ref, sb_ref, cv_ref, q_ref, kv_ref):
    xb = x_ref[...].astype(BF16)
    z = jnp.dot(xb, w_ref[...], preferred_element_type=F32)
    sb_ref[...] = z[:, OFF_SC_B:OFF_SC_B + D_SC]
    cv_ref[:, 0:D_SC] = z[:, OFF_SC_C:OFF_SC_C + D_SC] * z[:, OFF_SC_H:OFF_SC_H + D_SC]
    cv_ref[:, D_SC:D_SC + D_CC] = (z[:, OFF_CC_A:OFF_CC_A + D_CC]
                                   * jax.nn.sigmoid(z[:, OFF_CC_G:OFF_CC_G + D_CC]))
    cos = cos_ref[...]
    sin = sin_ref[...]
    lane = lax.broadcasted_iota(jnp.int32, cos.shape, 1)
    first_half = (lane % HEAD_DIM) < (HEAD_DIM // 2)
    low_head = lane < HEAD_DIM

    def rope(t):
        rot = jnp.where(first_half, pltpu.roll(t, LANES - HEAD_DIM // 2, 1),
                        pltpu.roll(t, HEAD_DIM // 2, 1))
        return t * cos + rot * sin

    def duplicate_heads(t, o_ref, off):
        swapped = pltpu.roll(t, HEAD_DIM, 1)
        o_ref[:, off:off + LANES] = jnp.where(low_head, t, swapped).astype(BF16)
        o_ref[:, off + LANES:off + 2 * LANES] = jnp.where(low_head, swapped, t).astype(BF16)

    for c in range(D_ATT // LANES):
        t = z[:, OFF_Q + c * LANES:OFF_Q + (c + 1) * LANES]
        q_ref[:, c * LANES:(c + 1) * LANES] = (rope(t) * (HEAD_DIM ** -0.5 * LOG2_E)).astype(BF16)
    duplicate_heads(rope(z[:, OFF_K:OFF_K + LANES]), kv_ref, 0)
    duplicate_heads(z[:, OFF_V:OFF_V + LANES], kv_ref, N_KV_HEADS * LANES)


def _inproj(x, w, cos, sin, seq, layer):
    rows = x.shape[0]
    tiles_per_seq = seq // INPROJ_ROWS
    row_spec = lambda width: pl.BlockSpec((INPROJ_ROWS, width), lambda i: (i, 0))
    pos_spec = pl.BlockSpec((INPROJ_ROWS, LANES), lambda i: (i % tiles_per_seq, 0))
    return pl.pallas_call(
        _inproj_body,
        grid=(rows // INPROJ_ROWS,),
        in_specs=[row_spec(D_MODEL), _layer_weight_spec(layer, (D_MODEL, D_IN)), pos_spec, pos_spec],
        out_specs=[row_spec(D_SC), row_spec(D_SC + D_CC), row_spec(D_ATT), row_spec(KV_WIDTH)],
        out_shape=[
            jax.ShapeDtypeStruct((rows, D_SC), F32),
            jax.ShapeDtypeStruct((rows, D_SC + D_CC), F32),
            jax.ShapeDtypeStruct((rows, D_ATT), BF16),
            jax.ShapeDtypeStruct((rows, KV_WIDTH), BF16),
        ],
        compiler_params=pltpu.CompilerParams(
            dimension_semantics=("arbitrary",), vmem_limit_bytes=VMEM_LIMIT_BYTES),
        name="inproj",
    )(x, w, cos, sin)


def _mixer_body(sink_ref, sb_ref, cvp_ref, cv_ref, cvn_ref, q_ref, kvp_ref, kv_ref, kvn_ref,
                scw_ref, ccw_ref, ccb_ref, ccg_ref, ccbeta_ref, wout_ref,
                o_ref, cvx_ref, shf_ref, kvx_ref, ycat_ref, *, tiles_per_seq):
    it = pl.program_id(0) % tiles_per_seq
    first = it == 0
    last = it == tiles_per_seq - 1
    n_blocks = MIX_ROWS // BLOCK
    ext_rows = MIX_ROWS + 2 * HALO_ROWS

    cvx_ref[0:HALO_ROWS, :] = jnp.where(first, 0.0, cvp_ref[...])
    cvx_ref[HALO_ROWS:HALO_ROWS + MIX_ROWS, :] = cv_ref[...]
    cvx_ref[HALO_ROWS + MIX_ROWS:, :] = jnp.where(last, 0.0, cvn_ref[...])
    ext_u = cvx_ref[:, D_SC:]
    for r in range(1, SUBLANES):
        shf_ref[r - 1, :, D_SC:] = pltpu.roll(ext_u, ext_rows - r, 0)
    ext_p = cvx_ref[:, :D_SC]
    for r in sorted({(HALO_ROWS + k - 1) % SUBLANES for k in range(SC_WIDTH)} - {0}):
        shf_ref[r - 1, :, :D_SC] = pltpu.roll(ext_p, ext_rows - r, 0)

    def shifted(row, lo_lane, hi_lane):
        r = row % SUBLANES
        if r == 0:
            return cvx_ref[row:row + CONV_ROWS, lo_lane:hi_lane]
        return shf_ref[r - 1, row - r:row - r + CONV_ROWS, lo_lane:hi_lane]

    lane = lax.broadcasted_iota(jnp.int32, (BLOCK, LANES), 1)
    low = lane < HEAD_DIM
    for src_ref, row0, n in ((kvp_ref, 0, BLOCK), (kv_ref, BLOCK, MIX_ROWS),
                             (kvn_ref, BLOCK + MIX_ROWS, BLOCK)):
        for rb in range(n // BLOCK):
            for c in range(KV_WIDTH // LANES):
                blk = src_ref[rb * BLOCK:(rb + 1) * BLOCK, c * LANES:(c + 1) * LANES]
                rows = slice(row0 + rb * BLOCK, row0 + (rb + 1) * BLOCK)
                kvx_ref[rows, 2 * c * LANES:(2 * c + 1) * LANES] = jnp.where(low, blk, 0)
                kvx_ref[rows, (2 * c + 1) * LANES:(2 * c + 2) * LANES] = jnp.where(low, 0, blk)

    ccb = ccb_ref[...]
    ccg = ccg_ref[...]
    ccbeta = ccbeta_ref[...]

    def tap_weight(w_ref, k):
        return jnp.tile(w_ref[k], (CONV_ROWS // SUBLANES, 1))

    def conv_chunk(rc, after):
        base = HALO_ROWS + rc * CONV_ROWS
        acc = None
        for k in range(SC_WIDTH):
            tap = shifted(base + k - 1, 0, D_SC) * tap_weight(scw_ref, k)
            acc = tap if acc is None else acc + tap
        y_sc = sb_ref[rc * CONV_ROWS:(rc + 1) * CONV_ROWS, :] * acc
        ycat_ref[rc * CONV_ROWS:(rc + 1) * CONV_ROWS, 0:D_SC] = y_sc.astype(BF16)
        acc = None
        for k in range(CC_WIDTH):
            tap = shifted(base + k - CC_PAD, D_SC, D_SC + D_CC) * tap_weight(ccw_ref, k)
            if k == 0 and after is not None:
                tap = tap + _zeros_from(after)
            acc = tap if acc is None else acc + tap
        u = _layer_norm(acc + ccb, ccg, ccbeta)
        y_cc = u * jax.nn.sigmoid(u)
        ycat_ref[rc * CONV_ROWS:(rc + 1) * CONV_ROWS, D_SC + D_ATT:] = y_cc.astype(BF16)
        return acc, _fold_to_vreg(y_sc) + _fold_to_vreg(y_cc)

    qi = lax.broadcasted_iota(jnp.int32, (BLOCK, BLOCK), 0)
    kj = lax.broadcasted_iota(jnp.int32, (BLOCK, BLOCK), 1)
    band_prev = kj >= qi
    band_next = kj <= qi
    low_pair = lax.broadcasted_iota(jnp.int32, (BLOCK, LANES), 1) < HEAD_DIM
    contract_lanes = (((1,), (1,)), ((), ()))

    def attention(j, kvh, anchor):
        valid_prev = band_prev & jnp.logical_not(first) if j == 0 else band_prev
        valid_next = band_next & jnp.logical_not(last) if j == n_blocks - 1 else band_next
        if anchor is None:
            masked = jnp.full((BLOCK, BLOCK), MASK_VALUE, F32)
        else:
            masked = jnp.tile(MASK_VALUE + _zeros_from(anchor), (BLOCK // SUBLANES, 1))
        q_rows = slice(j * BLOCK, (j + 1) * BLOCK)
        k_rows = slice(j * BLOCK, (j + 3) * BLOCK)
        qp = jnp.concatenate(
            [q_ref[q_rows, (2 * kvh) * LANES:(2 * kvh + 1) * LANES],
             q_ref[q_rows, (2 * kvh + 1) * LANES:(2 * kvh + 2) * LANES]], axis=0)
        probs = []
        rinv = []
        for parity in range(2):
            k_blk = kvx_ref[k_rows, (2 * kvh + parity) * LANES:(2 * kvh + parity + 1) * LANES]
            s_all = lax.dot_general(qp, k_blk, contract_lanes, preferred_element_type=F32)
            p_slabs = []
            for pair in range(2):
                h = GQA_GROUP * kvh + 2 * pair + parity
                rows = slice(pair * BLOCK, (pair + 1) * BLOCK)
                s_prev = jnp.where(valid_prev, s_all[rows, 0:BLOCK], masked)
                s_own = s_all[rows, BLOCK:2 * BLOCK]
                s_next = jnp.where(valid_next, s_all[rows, 2 * BLOCK:3 * BLOCK], masked)
                sink = sink_ref[h] * LOG2_E
                row_max = jnp.max(jnp.maximum(jnp.maximum(s_prev, s_own), s_next), axis=-1, keepdims=True)
                m = jnp.maximum(row_max, sink)
                p_prev = jnp.exp2(s_prev - m)
                p_own = jnp.exp2(s_own - m)
                p_next = jnp.exp2(s_next - m)
                denom = jnp.sum(p_prev + p_own + p_next, axis=-1, keepdims=True) + jnp.exp2(sink - m)
                rinv.append(1.0 / denom)
                p_slabs.append(jnp.concatenate(
                    [p_prev.astype(BF16), p_own.astype(BF16), p_next.astype(BF16)], axis=1))
            probs.append(jnp.concatenate(p_slabs, axis=0))
        v_off = 2 * N_KV_HEADS * LANES + 2 * kvh * LANES
        o = (jnp.dot(probs[0], kvx_ref[k_rows, v_off:v_off + LANES], preferred_element_type=F32)
             + jnp.dot(probs[1], kvx_ref[k_rows, v_off + LANES:v_off + 2 * LANES],
                       preferred_element_type=F32))
        for pair in range(2):
            scale = jnp.where(low_pair, rinv[pair], rinv[2 + pair])
            col = D_SC + (2 * kvh + pair) * LANES
            ycat_ref[q_rows, col:col + LANES] = (
                o[pair * BLOCK:(pair + 1) * BLOCK, :] * scale).astype(BF16)
        return _fold_to_vreg(o)

    n_steps = n_blocks * N_KV_HEADS
    chunks_per_step = MIX_ROWS // CONV_ROWS // n_steps
    anchors = []
    acc = None
    for step in range(n_steps):
        anchor = anchors[step - ATT_LAG] if step >= ATT_LAG else None
        done = attention(step // N_KV_HEADS, step % N_KV_HEADS, anchor)
        for c in range(chunks_per_step):
            acc, stored = conv_chunk(step * chunks_per_step + c, acc)
            done = done + stored
        anchors.append(done)

    o_ref[...] = jnp.dot(ycat_ref[...], wout_ref[...], preferred_element_type=F32)


def _mixer(sb, cv, q, kv, sink, scw, ccw, ccb, ccg, ccbeta, wout, seq, layer):
    rows = sb.shape[0]
    tiles_per_seq = seq // MIX_ROWS
    n_tiles = rows // MIX_ROWS
    halo_per_tile = MIX_ROWS // HALO_ROWS
    blocks_per_tile = MIX_ROWS // BLOCK
    row_spec = lambda width: pl.BlockSpec((MIX_ROWS, width), lambda i: (i, 0))
    cv_prev = pl.BlockSpec((HALO_ROWS, D_SC + D_CC),
                           lambda i: (jnp.maximum(i * halo_per_tile - 1, 0), 0))
    cv_next = pl.BlockSpec((HALO_ROWS, D_SC + D_CC),
                           lambda i: (jnp.minimum((i + 1) * halo_per_tile, n_tiles * halo_per_tile - 1), 0))
    kv_prev = pl.BlockSpec((BLOCK, KV_WIDTH),
                           lambda i: (jnp.maximum(i * blocks_per_tile - 1, 0), 0))
    kv_next = pl.BlockSpec((BLOCK, KV_WIDTH),
                           lambda i: (jnp.minimum((i + 1) * blocks_per_tile, n_tiles * blocks_per_tile - 1), 0))
    ext_rows = MIX_ROWS + 2 * HALO_ROWS
    return pl.pallas_call(
        functools.partial(_mixer_body, tiles_per_seq=tiles_per_seq),
        grid=(n_tiles,),
        in_specs=[
            pl.BlockSpec(memory_space=pltpu.SMEM),
            row_spec(D_SC),
            cv_prev, row_spec(D_SC + D_CC), cv_next,
            row_spec(D_ATT),
            kv_prev, row_spec(KV_WIDTH), kv_next,
            _const_spec((SC_WIDTH, SUBLANES, D_SC)), _const_spec((CC_WIDTH, SUBLANES, D_CC)),
            _const_spec((1, D_CC)), _const_spec((1, D_CC)), _const_spec((1, D_CC)),
            _layer_weight_spec(layer, (D_MODEL, D_MODEL)),
        ],
        out_specs=row_spec(D_MODEL),
        out_shape=jax.ShapeDtypeStruct((rows, D_MODEL), F32),
        scratch_shapes=[
            pltpu.VMEM((ext_rows, D_SC + D_CC), F32),
            pltpu.VMEM((SUBLANES - 1, ext_rows, D_SC + D_CC), F32),
            pltpu.VMEM((MIX_ROWS + 2 * BLOCK, 2 * KV_WIDTH), BF16),
            pltpu.VMEM((MIX_ROWS, D_MODEL), BF16),
        ],
        compiler_params=pltpu.CompilerParams(
            dimension_semantics=("arbitrary",), vmem_limit_bytes=VMEM_LIMIT_BYTES),
        name="mixer",
    )(sink, sb, cv, cv, cv, q, kv, kv, kv, scw, ccw, ccb, ccg, ccbeta, wout)


def _rope_tables(seq):
    half = HEAD_DIM // 2
    inv_freq = ROPE_THETA ** (-jnp.arange(half, dtype=F32) / half)
    ang = jnp.arange(seq).astype(F32)[:, None] * inv_freq[None, :]
    cos = jnp.cos(ang)
    sin = jnp.sin(ang)
    reps = LANES // HEAD_DIM
    cos_t = jnp.tile(jnp.concatenate([cos, cos], axis=-1), (1, reps))
    sin_t = jnp.tile(jnp.concatenate([-sin, sin], axis=-1), (1, reps))
    return cos_t, sin_t


def kernel(x, ffn1_w_gu, ffn1_w_down, ln1_g, ln1_b, w_in, sc_conv_w, attn_sink, cc_conv_w, cc_conv_b,
           cc_ln_g, cc_ln_b, w_out, ln2_g, ln2_b, ffn2_w_gu, ffn2_w_down, ln3_g, ln3_b):
    batch, seq, d = x.shape
    assert d == D_MODEL and seq % MIX_ROWS == 0 and seq % INPROJ_ROWS == 0
    assert (batch * seq) % FFN_ROWS == 0
    cos_t, sin_t = _rope_tables(seq)
    row = lambda v: v.reshape(1, -1)
    h = x.reshape(batch * seq, d)
    wgu1, wd1, wgu2, wd2 = (w.astype(BF16) for w in (ffn1_w_gu, ffn1_w_down, ffn2_w_gu, ffn2_w_down))
    win, wout = w_in.astype(BF16), w_out.astype(BF16)
    taps = lambda w: jnp.broadcast_to(w[:, None, :], (w.shape[0], SUBLANES, w.shape[1]))
    for l in range(DEPTH):
        h = _ffn(h, wgu1, wd1, row(ln1_g[l]), row(ln1_b[l]), l)
        sb, cv, q, kv = _inproj(h, win, cos_t, sin_t, seq, l)
        mixed = _mixer(sb, cv, q, kv, attn_sink[l], taps(sc_conv_w[l]), taps(cc_conv_w[l]),
                       row(cc_conv_b[l]), row(cc_ln_g[l]), row(cc_ln_b[l]), wout, seq, l)
        h = _ffn_prenorm(h, mixed, row(ln2_g[l]), row(ln2_b[l]), wgu2, wd2,
                         row(ln3_g[l]), row(ln3_b[l]), l)
    return h.reshape(batch, seq, d)
```

```python
import functools

import jax
import jax.numpy as jnp
from jax import lax
from jax.experimental import pallas as pl
from jax.experimental.pallas import tpu as pltpu

D_MODEL = 1024
DEPTH = 2
HEAD_DIM = 64
D_SC = 256
D_ATT = 512
D_CC = 256
N_Q_HEADS = 8
N_KV_HEADS = 2
GQA_GROUP = N_Q_HEADS // N_KV_HEADS
SC_WIDTH = 3
CC_WIDTH = 31
CC_PAD = (CC_WIDTH - 1) // 2
WINDOW = 128
BLOCK = 128
ROPE_THETA = 10000.0
D_FF = 2816
LN_EPS = 1e-5
ALPHA = (2.0 * DEPTH) ** 0.25
D_IN = 3 * D_SC + D_ATT + 2 * N_KV_HEADS * HEAD_DIM + 2 * D_CC
MASK_VALUE = -1e30

OFF_SC_B = 0
OFF_SC_C = D_SC
OFF_SC_H = 2 * D_SC
OFF_Q = 3 * D_SC
OFF_K = OFF_Q + D_ATT
OFF_V = OFF_K + N_KV_HEADS * HEAD_DIM
OFF_CC_A = OFF_V + N_KV_HEADS * HEAD_DIM
OFF_CC_G = OFF_CC_A + D_CC

LANES = 128
SUBLANES = 8
HALO_ROWS = 16
VMEM_LIMIT_BYTES = 56 * 1024 * 1024

FFN_ROWS = 512
INPROJ_ROWS = 1024
FFN_CHUNK = 256
N_CHUNKS = D_FF // FFN_CHUNK
OUT_NORM_CHUNKS = (1, 2, 3, 4)
PRE_NORM_CHUNKS = (5, 6, 7, 8)
MIX_ROWS = 1024
CONV_ROWS = 8
KV_WIDTH = 2 * N_KV_HEADS * LANES
ATT_LAG = 1
LOG2_E = 1.4426950408889634

F32 = jnp.float32
BF16 = jnp.bfloat16

assert WINDOW == BLOCK
assert CC_PAD <= HALO_ROWS and MIX_ROWS % BLOCK == 0 and FFN_ROWS % len(OUT_NORM_CHUNKS) == 0


def _layer_norm(y, g, b):
    mu = jnp.mean(y, axis=-1, keepdims=True)
    yc = y - mu
    var = jnp.mean(yc * yc, axis=-1, keepdims=True)
    return yc * lax.rsqrt(var + LN_EPS) * g + b


def _const_spec(shape):
    return pl.BlockSpec(shape, lambda i: (0,) * len(shape))


def _layer_weight_spec(layer, shape):
    return pl.BlockSpec((None,) + shape, lambda i: (layer,) + (0,) * len(shape),
                        pipeline_mode=pl.Buffered(1))


def _zeros_from(anchor):
    bits = lax.bitcast_convert_type(anchor, jnp.uint32)
    zero = lax.shift_right_logical(lax.shift_right_logical(bits, jnp.uint32(16)), jnp.uint32(16))
    return lax.bitcast_convert_type(zero, F32)


def _fold_to_vreg(v):
    acc = v[:, 0:LANES]
    for c in range(1, v.shape[1] // LANES):
        acc = acc + v[:, c * LANES:(c + 1) * LANES]
    out = acc[0:SUBLANES]
    for r in range(1, v.shape[0] // SUBLANES):
        out = out + acc[r * SUBLANES:(r + 1) * SUBLANES]
    return out


def _lane_block_sum(v):
    acc = v[:, 0:LANES]
    for c in range(1, v.shape[1] // LANES):
        acc = acc + v[:, c * LANES:(c + 1) * LANES]
    return acc


def _swiglu_residual(x, wgu_ref, wd_ref, anchors=None):
    xb = x.astype(BF16)
    acc = None
    for c in range(N_CHUNKS):
        g = jnp.dot(xb, wgu_ref[:, c * FFN_CHUNK:(c + 1) * FFN_CHUNK], preferred_element_type=F32)
        u = jnp.dot(xb, wgu_ref[:, D_FF + c * FFN_CHUNK:D_FF + (c + 1) * FFN_CHUNK],
                    preferred_element_type=F32)
        h = g * jax.nn.sigmoid(g) * u
        if anchors and c in anchors:
            zeros = _zeros_from(anchors[c])
            h = h + jnp.tile(zeros, (h.shape[0] // zeros.shape[0], FFN_CHUNK // LANES))
        d = jnp.dot(h.astype(BF16), wd_ref[c * FFN_CHUNK:(c + 1) * FFN_CHUNK, :],
                    preferred_element_type=F32)
        acc = d if acc is None else acc + d
    return ALPHA * x + 0.5 * acc


def _in_row_parts(compute, dst_ref, chunks):
    part = FFN_ROWS // len(chunks)
    anchors = {}
    for p, c in enumerate(chunks):
        rows = slice(p * part, (p + 1) * part)
        value = compute(rows)
        dst_ref[rows, :] = value
        anchors[c] = _lane_block_sum(value)
    return anchors


def _ffn_body(x_ref, wgu_ref, wd_ref, g_ref, b_ref, o_ref, y_ref):
    i = pl.program_id(0)
    n_tiles = pl.num_programs(0) - 1
    out_norm = lambda rows: _layer_norm(y_ref[rows, :], g_ref[...], b_ref[...])

    @pl.when(i == 0)
    def _():
        y_ref[...] = _swiglu_residual(x_ref[...], wgu_ref, wd_ref)

    @pl.when((i > 0) & (i < n_tiles))
    def _():
        anchors = _in_row_parts(out_norm, o_ref, OUT_NORM_CHUNKS)
        y_ref[...] = _swiglu_residual(x_ref[...], wgu_ref, wd_ref, anchors)

    @pl.when(i == n_tiles)
    def _():
        o_ref[...] = out_norm(slice(None))


def _ffn_prenorm_body(x_ref, m_ref, pg_ref, pb_ref, wgu_ref, wd_ref, g_ref, b_ref, o_ref, xn_ref, y_ref):
    j = pl.program_id(0)
    n_tiles = pl.num_programs(0) - 2
    out_norm = lambda rows: _layer_norm(y_ref[rows, :], g_ref[...], b_ref[...])
    pre_norm = lambda rows: _layer_norm(ALPHA * x_ref[rows, :] + m_ref[rows, :], pg_ref[...], pb_ref[...])

    @pl.when(j == 0)
    def _():
        xn_ref[0] = pre_norm(slice(None))
        y_ref[...] = jnp.zeros((FFN_ROWS, D_MODEL), F32)

    @pl.when((j > 0) & (j <= n_tiles))
    def _():
        x_cur = xn_ref[(j - 1) % 2]
        anchors = _in_row_parts(out_norm, o_ref, OUT_NORM_CHUNKS)
        anchors.update(_in_row_parts(pre_norm, xn_ref.at[j % 2], PRE_NORM_CHUNKS))
        y_ref[...] = _swiglu_residual(x_cur, wgu_ref, wd_ref, anchors)

    @pl.when(j == n_tiles + 1)
    def _():
        o_ref[...] = out_norm(slice(None))


def _ffn(x, wgu, wd, g, b, layer):
    rows = x.shape[0]
    n_tiles = rows // FFN_ROWS
    return pl.pallas_call(
        _ffn_body,
        grid=(n_tiles + 1,),
        in_specs=[
            pl.BlockSpec((FFN_ROWS, D_MODEL), lambda i: (jnp.minimum(i, n_tiles - 1), 0)),
            _layer_weight_spec(layer, (D_MODEL, 2 * D_FF)),
            _layer_weight_spec(layer, (D_FF, D_MODEL)),
            _const_spec((1, D_MODEL)),
            _const_spec((1, D_MODEL)),
        ],
        out_specs=pl.BlockSpec((FFN_ROWS, D_MODEL), lambda i: (jnp.maximum(i - 1, 0), 0)),
        out_shape=jax.ShapeDtypeStruct((rows, D_MODEL), F32),
        scratch_shapes=[pltpu.VMEM((FFN_ROWS, D_MODEL), F32)],
        compiler_params=pltpu.CompilerParams(
            dimension_semantics=("arbitrary",), vmem_limit_bytes=VMEM_LIMIT_BYTES),
        name="ffn",
    )(x, wgu, wd, g, b)


def _ffn_prenorm(x, mixed, pre_g, pre_b, wgu, wd, g, b, layer):
    rows = x.shape[0]
    n_tiles = rows // FFN_ROWS
    in_tile = pl.BlockSpec((FFN_ROWS, D_MODEL), lambda j: (jnp.minimum(j, n_tiles - 1), 0))
    return pl.pallas_call(
        _ffn_prenorm_body,
        grid=(n_tiles + 2,),
        in_specs=[
            in_tile, in_tile,
            _const_spec((1, D_MODEL)), _const_spec((1, D_MODEL)),
            _layer_weight_spec(layer, (D_MODEL, 2 * D_FF)),
            _layer_weight_spec(layer, (D_FF, D_MODEL)),
            _const_spec((1, D_MODEL)), _const_spec((1, D_MODEL)),
        ],
        out_specs=pl.BlockSpec((FFN_ROWS, D_MODEL), lambda j: (jnp.clip(j - 2, 0, n_tiles - 1), 0)),
        out_shape=jax.ShapeDtypeStruct((rows, D_MODEL), F32),
        scratch_shapes=[pltpu.VMEM((2, FFN_ROWS, D_MODEL), F32), pltpu.VMEM((FFN_ROWS, D_MODEL), F32)],
        compiler_params=pltpu.CompilerParams(
            dimension_semantics=("arbitrary",), vmem_limit_bytes=VMEM_LIMIT_BYTES),
        name="ffn_prenorm",
    )(x, mixed, pre_g, pre_b, wgu, wd, g, b)


def _inproj_body(x_ref, w_ref, cos_ref, sin_ref, sb_ref, cv_ref, q_ref, kv_ref):
    xb = x_ref[...].astype(BF16)
    z = jnp.dot(xb, w_ref[...], preferred_element_type=F32)
    sb_ref[...] = z[:, OFF_SC_B:OFF_SC_B + D_SC]
    cv_ref[:, 0:D_SC] = z[:, OFF_SC_C:OFF_SC_C + D_SC] * z[:, OFF_SC_H:OFF_SC_H + D_SC]
    cv_ref[:, D_SC:D_SC + D_CC] = (z[:, OFF_CC_A:OFF_CC_A + D_CC]
                                   * jax.nn.sigmoid(z[:, OFF_CC_G:OFF_CC_G + D_CC]))
    cos = cos_ref[...]
    sin = sin_ref[...]
    lane = lax.broadcasted_iota(jnp.int32, cos.shape, 1)
    first_half = (lane % HEAD_DIM) < (HEAD_DIM // 2)
    low_head = lane < HEAD_DIM

    def rope(t):
        rot = jnp.where(first_half, pltpu.roll(t, LANES - HEAD_DIM // 2, 1),
                        pltpu.roll(t, HEAD_DIM // 2, 1))
        return t * cos + rot * sin

    def duplicate_heads(t, o_ref, off):
        swapped = pltpu.roll(t, HEAD_DIM, 1)
        o_ref[:, off:off + LANES] = jnp.where(low_head, t, swapped).astype(BF16)
        o_ref[:, off + LANES:off + 2 * LANES] = jnp.where(low_head, swapped, t).astype(BF16)

    for c in range(D_ATT // LANES):
        t = z[:, OFF_Q + c * LANES:OFF_Q + (c + 1) * LANES]
        q_ref[:, c * LANES:(c + 1) * LANES] = (rope(t) * (HEAD_DIM ** -0.5 * LOG2_E)).astype(BF16)
    duplicate_heads(rope(z[:, OFF_K:OFF_K + LANES]), kv_ref, 0)
    duplicate_heads(z[:, OFF_V:OFF_V + LANES], kv_ref, N_KV_HEADS * LANES)


def _inproj(x, w, cos, sin, seq, layer):
    rows = x.shape[0]
    tiles_per_seq = seq // INPROJ_ROWS
    row_spec = lambda width: pl.BlockSpec((INPROJ_ROWS, width), lambda i: (i, 0))
    pos_spec = pl.BlockSpec((INPROJ_ROWS, LANES), lambda i: (i % tiles_per_seq, 0))
    return pl.pallas_call(
        _inproj_body,
        grid=(rows // INPROJ_ROWS,),
        in_specs=[row_spec(D_MODEL), _layer_weight_spec(layer, (D_MODEL, D_IN)), pos_spec, pos_spec],
        out_specs=[row_spec(D_SC), row_spec(D_SC + D_CC), row_spec(D_ATT), row_spec(KV_WIDTH)],
        out_shape=[
            jax.ShapeDtypeStruct((rows, D_SC), F32),
            jax.ShapeDtypeStruct((rows, D_SC + D_CC), F32),
            jax.ShapeDtypeStruct((rows, D_ATT), BF16),
            jax.ShapeDtypeStruct((rows, KV_WIDTH), BF16),
        ],
        compiler_params=pltpu.CompilerParams(
            dimension_semantics=("arbitrary",), vmem_limit_bytes=VMEM_LIMIT_BYTES),
        name="inproj",
    )(x, w, cos, sin)


def _mixer_body(sink_ref, sb_ref, cvp_ref, cv_ref, cvn_ref, q_ref, kvp_ref, kv_ref, kvn_ref,
                scw_ref, ccw_ref, ccb_ref, ccg_ref, ccbeta_ref, wout_ref,
                o_ref, cvx_ref, shf_ref, kvx_ref, ycat_ref, *, tiles_per_seq):
    it = pl.program_id(0) % tiles_per_seq
    first = it == 0
    last = it == tiles_per_seq - 1
    n_blocks = MIX_ROWS // BLOCK
    ext_rows = MIX_ROWS + 2 * HALO_ROWS

    cvx_ref[0:HALO_ROWS, :] = jnp.where(first, 0.0, cvp_ref[...])
    cvx_ref[HALO_ROWS:HALO_ROWS + MIX_ROWS, :] = cv_ref[...]
    cvx_ref[HALO_ROWS + MIX_ROWS:, :] = jnp.where(last, 0.0, cvn_ref[...])
    ext_u = cvx_ref[:, D_SC:]
    for r in range(1, SUBLANES):
        shf_ref[r - 1, :, D_SC:] = pltpu.roll(ext_u, ext_rows - r, 0)
    ext_p = cvx_ref[:, :D_SC]
    for r in sorted({(HALO_ROWS + k - 1) % SUBLANES for k in range(SC_WIDTH)} - {0}):
        shf_ref[r - 1, :, :D_SC] = pltpu.roll(ext_p, ext_rows - r, 0)

    def shifted(row, lo_lane, hi_lane):
        r = row % SUBLANES
        if r == 0:
            return cvx_ref[row:row + CONV_ROWS, lo_lane:hi_lane]
        return shf_ref[r - 1, row - r:row - r + CONV_ROWS, lo_lane:hi_lane]

    lane = lax.broadcasted_iota(jnp.int32, (BLOCK, LANES), 1)
    low = lane < HEAD_DIM
    for src_ref, row0, n in ((kvp_ref, 0, BLOCK), (kv_ref, BLOCK, MIX_ROWS),
                             (kvn_ref, BLOCK + MIX_ROWS, BLOCK)):
        for rb in range(n // BLOCK):
            for c in range(KV_WIDTH // LANES):
                blk = src_ref[rb * BLOCK:(rb + 1) * BLOCK, c * LANES:(c + 1) * LANES]
                rows = slice(row0 + rb * BLOCK, row0 + (rb + 1) * BLOCK)
                kvx_ref[rows, 2 * c * LANES:(2 * c + 1) * LANES] = jnp.where(low, blk, 0)
                kvx_ref[rows, (2 * c + 1) * LANES:(2 * c + 2) * LANES] = jnp.where(low, 0, blk)

    ccb = ccb_ref[...]
    ccg = ccg_ref[...]
    ccbeta = ccbeta_ref[...]

    def tap_weight(w_ref, k):
        return jnp.tile(w_ref[k], (CONV_ROWS // SUBLANES, 1))

    def conv_chunk(rc, after):
        base = HALO_ROWS + rc * CONV_ROWS
        acc = None
        for k in range(SC_WIDTH):
            tap = shifted(base + k - 1, 0, D_SC) * tap_weight(scw_ref, k)
            acc = tap if acc is None else acc + tap
        y_sc = sb_ref[rc * CONV_ROWS:(rc + 1) * CONV_ROWS, :] * acc
        ycat_ref[rc * CONV_ROWS:(rc + 1) * CONV_ROWS, 0:D_SC] = y_sc.astype(BF16)
        acc = None
        for k in range(CC_WIDTH):
            tap = shifted(base + k - CC_PAD, D_SC, D_SC + D_CC) * tap_weight(ccw_ref, k)
            if k == 0 and after is not None:
                tap = tap + _zeros_from(after)
            acc = tap if acc is None else acc + tap
        u = _layer_norm(acc + ccb, ccg, ccbeta)
        y_cc = u * jax.nn.sigmoid(u)
        ycat_ref[rc * CONV_ROWS:(rc + 1) * CONV_ROWS, D_SC + D_ATT:] = y_cc.astype(BF16)
        return acc, _fold_to_vreg(y_sc) + _fold_to_vreg(y_cc)

    qi = lax.broadcasted_iota(jnp.int32, (BLOCK, BLOCK), 0)
    kj = lax.broadcasted_iota(jnp.int32, (BLOCK, BLOCK), 1)
    band_prev = kj >= qi
    band_next = kj <= qi
    low_pair = lax.broadcasted_iota(jnp.int32, (BLOCK, LANES), 1) < HEAD_DIM
    contract_lanes = (((1,), (1,)), ((), ()))

    def attention(j, kvh, anchor):
        valid_prev = band_prev & jnp.logical_not(first) if j == 0 else band_prev
        valid_next = band_next & jnp.logical_not(last) if j == n_blocks - 1 else band_next
        if anchor is None:
            masked = jnp.full((BLOCK, BLOCK), MASK_VALUE, F32)
        else:
            masked = jnp.tile(MASK_VALUE + _zeros_from(anchor), (BLOCK // SUBLANES, 1))
        q_rows = slice(j * BLOCK, (j + 1) * BLOCK)
        k_rows = slice(j * BLOCK, (j + 3) * BLOCK)
        qp = jnp.concatenate(
            [q_ref[q_rows, (2 * kvh) * LANES:(2 * kvh + 1) * LANES],
             q_ref[q_rows, (2 * kvh + 1) * LANES:(2 * kvh + 2) * LANES]], axis=0)
        probs = []
        rinv = []
        for parity in range(2):
            k_blk = kvx_ref[k_rows, (2 * kvh + parity) * LANES:(2 * kvh + parity + 1) * LANES]
            s_all = lax.dot_general(qp, k_blk, contract_lanes, preferred_element_type=F32)
            p_slabs = []
            for pair in range(2):
                h = GQA_GROUP * kvh + 2 * pair + parity
                rows = slice(pair * BLOCK, (pair + 1) * BLOCK)
                s_prev = jnp.where(valid_prev, s_all[rows, 0:BLOCK], masked)
                s_own = s_all[rows, BLOCK:2 * BLOCK]
                s_next = jnp.where(valid_next, s_all[rows, 2 * BLOCK:3 * BLOCK], masked)
                sink = sink_ref[h] * LOG2_E
                row_max = jnp.max(jnp.maximum(jnp.maximum(s_prev, s_own), s_next), axis=-1, keepdims=True)
                m = jnp.maximum(row_max, sink)
                p_prev = jnp.exp2(s_prev - m)
                p_own = jnp.exp2(s_own - m)
                p_next = jnp.exp2(s_next - m)
                denom = jnp.sum(p_prev + p_own + p_next, axis=-1, keepdims=True) + jnp.exp2(sink - m)
                rinv.append(1.0 / denom)
                p_slabs.append(jnp.concatenate(
                    [p_prev.astype(BF16), p_own.astype(BF16), p_next.astype(BF16)], axis=1))
            probs.append(jnp.concatenate(p_slabs, axis=0))
        v_off = 2 * N_KV_HEADS * LANES + 2 * kvh * LANES
        o = (jnp.dot(probs[0], kvx_ref[k_rows, v_off:v_off + LANES], preferred_element_type=F32)
             + jnp.dot(probs[1], kvx_ref[k_rows, v_off + LANES:v_off + 2 * LANES],
                       preferred_element_type=F32))
        for pair in range(2):
            scale = jnp.where(low_pair, rinv[pair], rinv[2 + pair])
            col = D_SC + (2 * kvh + pair) * LANES
            ycat_ref[q_rows, col:col + LANES] = (
                o[pair * BLOCK:(pair + 1) * BLOCK, :] * scale).astype(BF16)

    n_steps = n_blocks * N_KV_HEADS
    chunks_per_step = MIX_ROWS // CONV_ROWS // n_steps
    anchors = []
    acc = None
    for step in range(n_steps):
        anchor = anchors[step - ATT_LAG] if step >= ATT_LAG else None
        attention(step // N_KV_HEADS, step % N_KV_HEADS, anchor)
        done = None
        for c in range(chunks_per_step):
            acc, stored = conv_chunk(step * chunks_per_step + c, acc)
            done = stored if done is None else done + stored
        anchors.append(done)

    o_ref[...] = jnp.dot(ycat_ref[...], wout_ref[...], preferred_element_type=F32)


def _mixer(sb, cv, q, kv, sink, scw, ccw, ccb, ccg, ccbeta, wout, seq, layer):
    rows = sb.shape[0]
    tiles_per_seq = seq // MIX_ROWS
    n_tiles = rows // MIX_ROWS
    halo_per_tile = MIX_ROWS // HALO_ROWS
    blocks_per_tile = MIX_ROWS // BLOCK
    row_spec = lambda width: pl.BlockSpec((MIX_ROWS, width), lambda i: (i, 0))
    cv_prev = pl.BlockSpec((HALO_ROWS, D_SC + D_CC),
                           lambda i: (jnp.maximum(i * halo_per_tile - 1, 0), 0))
    cv_next = pl.BlockSpec((HALO_ROWS, D_SC + D_CC),
                           lambda i: (jnp.minimum((i + 1) * halo_per_tile, n_tiles * halo_per_tile - 1), 0))
    kv_prev = pl.BlockSpec((BLOCK, KV_WIDTH),
                           lambda i: (jnp.maximum(i * blocks_per_tile - 1, 0), 0))
    kv_next = pl.BlockSpec((BLOCK, KV_WIDTH),
                           lambda i: (jnp.minimum((i + 1) * blocks_per_tile, n_tiles * blocks_per_tile - 1), 0))
    ext_rows = MIX_ROWS + 2 * HALO_ROWS
    return pl.pallas_call(
        functools.partial(_mixer_body, tiles_per_seq=tiles_per_seq),
        grid=(n_tiles,),
        in_specs=[
            pl.BlockSpec(memory_space=pltpu.SMEM),
            row_spec(D_SC),
            cv_prev, row_spec(D_SC + D_CC), cv_next,
            row_spec(D_ATT),
            kv_prev, row_spec(KV_WIDTH), kv_next,
            _const_spec((SC_WIDTH, SUBLANES, D_SC)), _const_spec((CC_WIDTH, SUBLANES, D_CC)),
            _const_spec((1, D_CC)), _const_spec((1, D_CC)), _const_spec((1, D_CC)),
            _layer_weight_spec(layer, (D_MODEL, D_MODEL)),
        ],
        out_specs=row_spec(D_MODEL),
        out_shape=jax.ShapeDtypeStruct((rows, D_MODEL), F32),
        scratch_shapes=[
            pltpu.VMEM((ext_rows, D_SC + D_CC), F32),
            pltpu.VMEM((SUBLANES - 1, ext_rows, D_SC + D_CC), F32),
            pltpu.VMEM((MIX_ROWS + 2 * BLOCK, 2 * KV_WIDTH), BF16),
            pltpu.VMEM((MIX_ROWS, D_MODEL), BF16),
        ],
        compiler_params=pltpu.CompilerParams(
            dimension_semantics=("arbitrary",), vmem_limit_bytes=VMEM_LIMIT_BYTES),
        name="mixer",
    )(sink, sb, cv, cv, cv, q, kv, kv, kv, scw, ccw, ccb, ccg, ccbeta, wout)


def _rope_tables(seq):
    half = HEAD_DIM // 2
    inv_freq = ROPE_THETA ** (-jnp.arange(half, dtype=F32) / half)
    ang = jnp.arange(seq).astype(F32)[:, None] * inv_freq[None, :]
    cos = jnp.cos(ang)
    sin = jnp.sin(ang)
    reps = LANES // HEAD_DIM
    cos_t = jnp.tile(jnp.concatenate([cos, cos], axis=-1), (1, reps))
    sin_t = jnp.tile(jnp.concatenate([-sin, sin], axis=-1), (1, reps))
    return cos_t, sin_t


def kernel(x, ffn1_w_gu, ffn1_w_down, ln1_g, ln1_b, w_in, sc_conv_w, attn_sink, cc_conv_w, cc_conv_b,
           cc_ln_g, cc_ln_b, w_out, ln2_g, ln2_b, ffn2_w_gu, ffn2_w_down, ln3_g, ln3_b):
    batch, seq, d = x.shape
    assert d == D_MODEL and seq % MIX_ROWS == 0 and seq % INPROJ_ROWS == 0
    assert (batch * seq) % FFN_ROWS == 0
    cos_t, sin_t = _rope_tables(seq)
    row = lambda v: v.reshape(1, -1)
    h = x.reshape(batch * seq, d)
    wgu1, wd1, wgu2, wd2 = (w.astype(BF16) for w in (ffn1_w_gu, ffn1_w_down, ffn2_w_gu, ffn2_w_down))
    win, wout = w_in.astype(BF16), w_out.astype(BF16)
    taps = lambda w: jnp.broadcast_to(w[:, None, :], (w.shape[0], SUBLANES, w.shape[1]))
    for l in range(DEPTH):
        h = _ffn(h, wgu1, wd1, row(ln1_g[l]), row(ln1_b[l]), l)
        sb, cv, q, kv = _inproj(h, win, cos_t, sin_t, seq, l)
        mixed = _mixer(sb, cv, q, kv, attn_sink[l], taps(sc_conv_w[l]), taps(cc_conv_w[l]),
                       row(cc_conv_b[l]), row(cc_ln_g[l]), row(cc_ln_b[l]), wout, seq, l)
        h = _ffn_prenorm(h, mixed, row(ln2_g[l]), row(ln2_b[l]), wgu2, wd2,
                         row(ln3_g[l]), row(ln3_b[l]), l)
    return h.reshape(batch, seq, d)
```
